```python
import jax, jax.numpy as jnp
from jax import lax
import numpy as np

D_MODEL = 1024
BATCH = 4
SEQ = 4096
DEPTH = 2

N_GROUPS = 8
GROUP_DIM = D_MODEL // 16
MIX_W = N_GROUPS * GROUP_DIM
CONF_WIDTH = 31
LRU_CONV_WIDTH = 4
LRU_C = 8.0
SHORT_CONV_WIDTH = 3
MOBA_BLOCK = 256
MOBA_TOPK = 3
MOBA_QCHUNK = 32
ROPE_THETA = 10000.0
FFN_DIM = 2816
N_EXPERTS = 8
TOP_K = 2
EXPERT_DIM = 3584
EXPERT_ROWS = 256
EPS = 1e-6
N_EVEN = (DEPTH + 1) // 2
N_ODD = DEPTH // 2

kernel_name = "hybrid_conformer_rglru_shortconv_moba_moe"

F32 = jnp.float32


def rms_norm(x, g):
    x32 = x.astype(F32)
    y = x32 * lax.rsqrt(jnp.mean(x32 * x32, axis=-1, keepdims=True) + EPS)
    return (y * g.astype(F32)).astype(x.dtype)


def layer_norm(x, g, b):
    x32 = x.astype(F32)
    mu = jnp.mean(x32, axis=-1, keepdims=True)
    var = jnp.mean(jnp.square(x32 - mu), axis=-1, keepdims=True)
    y = (x32 - mu) * lax.rsqrt(var + EPS) * g.astype(F32) + b.astype(F32)
    return y.astype(x.dtype)


def causal_depthwise_conv(x, w, b=None):
    width, ch = w.shape
    xp = jnp.pad(x, ((0, 0), (width - 1, 0), (0, 0)))
    y = lax.conv_general_dilated(xp, w[:, None, :].astype(x.dtype), window_strides=(1,),
                                 padding='VALID', dimension_numbers=('NWC', 'WIO', 'NWC'),
                                 feature_group_count=ch)
    return y if b is None else y + b


def rope(x, pos):
    half = x.shape[-1] // 2
    inv = ROPE_THETA ** (-jnp.arange(half, dtype=F32) / half)
    ang = pos.astype(F32)[:, None] * inv[None, :]
    cos = jnp.cos(ang)[None, :, None, :]
    sin = jnp.sin(ang)[None, :, None, :]
    x32 = x.astype(F32)
    x1, x2 = x32[..., :half], x32[..., half:]
    return jnp.concatenate([x1 * cos - x2 * sin, x2 * cos + x1 * sin], axis=-1).astype(x.dtype)


def rg_lru(x, wa, ba, wx, bx, lam):
    bsz, s, ch = x.shape
    xg = x.reshape(bsz, s, N_GROUPS, GROUP_DIM)
    r = jax.nn.sigmoid(jnp.einsum('bshi,hij->bshj', xg, wa).reshape(bsz, s, ch) + ba)
    i = jax.nn.sigmoid(jnp.einsum('bshi,hij->bshj', xg, wx).reshape(bsz, s, ch) + bx)
    log_a = LRU_C * r.astype(F32) * jax.nn.log_sigmoid(lam.astype(F32))
    a = jnp.exp(log_a)
    mult = jnp.sqrt(-jnp.expm1(2.0 * log_a))
    bterm = mult * (i * x).astype(F32)

    def combine(left, right):
        a_l, b_l = left
        a_r, b_r = right
        return a_l * a_r, a_r * b_l + b_r

    _, h = lax.associative_scan(combine, (a, bterm), axis=1)
    return h.astype(x.dtype)


def mixer_conformer_rglru(h, w_in, conv_a_w, conv_a_b, ln_a_g, ln_a_b, conv_b_w, conv_b_b,
                          lru_wa, lru_ba, lru_wx, lru_bx, lru_lambda, w_out):
    u = h @ w_in
    a_val, a_gate, b_x, b_gate = jnp.split(u, 4, axis=-1)
    ya = a_val * jax.nn.sigmoid(a_gate)
    ya = causal_depthwise_conv(ya, conv_a_w, conv_a_b)
    ya = jax.nn.silu(layer_norm(ya, ln_a_g, ln_a_b))
    xb = causal_depthwise_conv(b_x, conv_b_w, conv_b_b)
    yb = rg_lru(xb, lru_wa, lru_ba, lru_wx, lru_bx, lru_lambda) * jax.nn.gelu(b_gate)
    return jnp.concatenate([ya, yb], axis=-1) @ w_out


def moba_attention(q, k, v):
    bsz, s, nh, dh = q.shape
    s_pad = -(-s // MOBA_BLOCK) * MOBA_BLOCK
    padw = ((0, 0), (0, s_pad - s), (0, 0), (0, 0))
    q, k, v = jnp.pad(q, padw), jnp.pad(k, padw), jnp.pad(v, padw)
    nb = s_pad // MOBA_BLOCK
    nc = s_pad // MOBA_QCHUNK
    n_topk = min(MOBA_TOPK, nb)
    scale = dh ** -0.5
    kb = k.transpose(0, 2, 1, 3).reshape(bsz, nh, nb, MOBA_BLOCK, dh)
    vb = v.transpose(0, 2, 1, 3).reshape(bsz, nh, nb, MOBA_BLOCK, dh)
    k_mean = jnp.mean(kb.astype(F32), axis=3)
    q_chunks = q.transpose(0, 2, 1, 3).reshape(bsz, nh, nc, MOBA_QCHUNK, dh).transpose(2, 0, 1, 3, 4)
    b_idx = jnp.arange(bsz)[:, None, None, None]
    h_idx = jnp.arange(nh)[None, :, None, None]
    blk_ids = jnp.arange(nb)
    slot = jnp.arange(n_topk)
    within = jnp.arange(MOBA_BLOCK)
    n_sel = n_topk * MOBA_BLOCK

    def one_chunk(args):
        qc, ci = args
        q_start = ci * MOBA_QCHUNK
        own = q_start // MOBA_BLOCK
        q_pos = q_start + jnp.arange(MOBA_QCHUNK)
        gate = jnp.einsum('bhqd,bhnd->bhqn', qc.astype(F32), k_mean)
        gate = jnp.where(blk_ids < own, gate, -jnp.inf)
        _, sel = lax.top_k(gate, n_topk)
        sel_valid = slot < own
        k_sel = kb[b_idx, h_idx, sel]
        v_sel = vb[b_idx, h_idx, sel]
        s_sel = jnp.einsum('bhqd,bhqkjd->bhqkj', qc, k_sel).astype(F32) * scale
        s_sel = jnp.where(sel_valid[:, None], s_sel, -jnp.inf)
        k_own = lax.dynamic_index_in_dim(kb, own, axis=2, keepdims=False)
        v_own = lax.dynamic_index_in_dim(vb, own, axis=2, keepdims=False)
        s_own = jnp.einsum('bhqd,bhjd->bhqj', qc, k_own).astype(F32) * scale
        causal = (own * MOBA_BLOCK + within)[None, :] <= q_pos[:, None]
        s_own = jnp.where(causal, s_own, -jnp.inf)
        scores = jnp.concatenate([s_sel.reshape(bsz, nh, MOBA_QCHUNK, n_sel), s_own], axis=-1)
        p = jax.nn.softmax(scores, axis=-1)
        p_sel = p[..., :n_sel].reshape(bsz, nh, MOBA_QCHUNK, n_topk, MOBA_BLOCK).astype(v.dtype)
        p_own = p[..., n_sel:].astype(v.dtype)
        return (jnp.einsum('bhqkj,bhqkjd->bhqd', p_sel, v_sel)
                + jnp.einsum('bhqj,bhjd->bhqd', p_own, v_own))

    out = lax.map(one_chunk, (q_chunks, jnp.arange(nc)))
    out = out.transpose(1, 0, 3, 2, 4).reshape(bsz, s_pad, nh, dh)
    return out[:, :s]


def mixer_shortconv_moba(h, w_in, conv_c_w, q_norm, k_norm, w_out, pos):
    bsz, s, _ = h.shape
    u = h @ w_in
    c_h, c_b, c_c, q, k, v = jnp.split(u, 6, axis=-1)
    yc = c_b * causal_depthwise_conv(c_c * c_h, conv_c_w)
    q = rope(rms_norm(q.reshape(bsz, s, N_GROUPS, GROUP_DIM), q_norm), pos)
    k = rope(rms_norm(k.reshape(bsz, s, N_GROUPS, GROUP_DIM), k_norm), pos)
    v = v.reshape(bsz, s, N_GROUPS, GROUP_DIM)
    yd = moba_attention(q, k, v).reshape(bsz, s, MIX_W)
    return jnp.concatenate([yc, yd], axis=-1) @ w_out


def swiglu(h, wg, wu, wd):
    return (jax.nn.silu(h @ wg) * (h @ wu)) @ wd


def moe_swiglu(h, router_w, router_b, wg, wu, wd):
    bsz, s, d = h.shape
    n_tok = bsz * s
    xf = h.reshape(n_tok, d)
    logits = xf.astype(F32) @ router_w.astype(F32) + router_b.astype(F32)
    top_logit, top_e = lax.top_k(logits, TOP_K)
    top_w = jax.nn.softmax(top_logit, axis=-1)
    n_assign = n_tok * TOP_K
    flat_e = top_e.reshape(-1)
    flat_tok = jnp.repeat(jnp.arange(n_tok, dtype=jnp.int32), TOP_K)
    flat_w = top_w.reshape(-1)
    order = jnp.argsort(flat_e)
    sorted_e = flat_e[order]
    counts = jnp.bincount(flat_e, length=N_EXPERTS)
    padded = (counts + EXPERT_ROWS - 1) // EXPERT_ROWS * EXPERT_ROWS
    start = jnp.cumsum(counts) - counts
    pend = jnp.cumsum(padded)
    pstart = pend - padded
    dest = pstart[sorted_e] + (jnp.arange(n_assign) - start[sorted_e])
    n_rows = -(-(n_assign + N_EXPERTS * (EXPERT_ROWS - 1)) // EXPERT_ROWS) * EXPERT_ROWS
    n_blocks = n_rows // EXPERT_ROWS
    row_tok = jnp.zeros((n_rows,), jnp.int32).at[dest].set(flat_tok[order])
    row_w = jnp.zeros((n_rows,), F32).at[dest].set(flat_w[order])
    blk_e = jnp.minimum(jnp.searchsorted(pend, jnp.arange(n_blocks) * EXPERT_ROWS, side='right'),
                        N_EXPERTS - 1)
    xs = xf[row_tok].reshape(n_blocks, EXPERT_ROWS, d)

    def expert_block(args):
        xb, e = args
        return (jax.nn.silu(xb @ wg[e]) * (xb @ wu[e])) @ wd[e]

    ys = lax.map(expert_block, (xs, blk_e)).reshape(n_rows, d)
    out = jnp.zeros((n_tok, d), F32).at[row_tok].add(ys.astype(F32) * row_w[:, None])
    return out.astype(h.dtype).reshape(bsz, s, d)


def setup_inputs(seed: int = 0) -> dict:
    key = jax.random.key(seed)
    ks = iter(jax.random.split(key, 48))
    D = D_MODEL

    def nrm(shape, scale):
        return jax.random.normal(next(ks), shape, jnp.float32) * scale

    a0 = jax.random.uniform(next(ks), (N_EVEN, MIX_W), jnp.float32, 0.9, 0.999)
    return {
        "x": nrm((BATCH, SEQ, D), 1.0),
        "c": nrm((BATCH, D), 1.0),
        "e_ada_w": nrm((N_EVEN, D, 6 * D), 0.5 * D ** -0.5),
        "e_ada_b": nrm((N_EVEN, 6 * D), 0.02),
        "e_norm_mix": 1.0 + nrm((N_EVEN, D), 0.05),
        "e_norm_ffn": 1.0 + nrm((N_EVEN, D), 0.05),
        "e_w_in": nrm((N_EVEN, D, 4 * MIX_W), D ** -0.5),
        "e_conv_a_w": nrm((N_EVEN, CONF_WIDTH, MIX_W), CONF_WIDTH ** -0.5),
        "e_conv_a_b": nrm((N_EVEN, MIX_W), 0.02),
        "e_ln_a_g": 1.0 + nrm((N_EVEN, MIX_W), 0.05),
        "e_ln_a_b": nrm((N_EVEN, MIX_W), 0.02),
        "e_conv_b_w": nrm((N_EVEN, LRU_CONV_WIDTH, MIX_W), LRU_CONV_WIDTH ** -0.5),
        "e_conv_b_b": nrm((N_EVEN, MIX_W), 0.02),
        "e_lru_wa": nrm((N_EVEN, N_GROUPS, GROUP_DIM, GROUP_DIM), GROUP_DIM ** -0.5),
        "e_lru_ba": nrm((N_EVEN, MIX_W), 0.02),
        "e_lru_wx": nrm((N_EVEN, N_GROUPS, GROUP_DIM, GROUP_DIM), GROUP_DIM ** -0.5),
        "e_lru_bx": nrm((N_EVEN, MIX_W), 0.02),
        "e_lru_lambda": jnp.log(a0) - jnp.log1p(-a0),
        "e_w_out": nrm((N_EVEN, 2 * MIX_W, D), (2 * MIX_W) ** -0.5),
        "e_ffn_wg": nrm((N_EVEN, D, FFN_DIM), D ** -0.5),
        "e_ffn_wu": nrm((N_EVEN, D, FFN_DIM), D ** -0.5),
        "e_ffn_wd": nrm((N_EVEN, FFN_DIM, D), FFN_DIM ** -0.5),
        "o_ada_w": nrm((N_ODD, D, 6 * D), 0.5 * D ** -0.5),
        "o_ada_b": nrm((N_ODD, 6 * D), 0.02),
        "o_norm_mix": 1.0 + nrm((N_ODD, D), 0.05),
        "o_norm_ffn": 1.0 + nrm((N_ODD, D), 0.05),
        "o_w_in": nrm((N_ODD, D, 6 * MIX_W), D ** -0.5),
        "o_conv_c_w": nrm((N_ODD, SHORT_CONV_WIDTH, MIX_W), SHORT_CONV_WIDTH ** -0.5),
        "o_q_norm": 1.0 + nrm((N_ODD, GROUP_DIM), 0.05),
        "o_k_norm": 1.0 + nrm((N_ODD, GROUP_DIM), 0.05),
        "o_w_out": nrm((N_ODD, 2 * MIX_W, D), (2 * MIX_W) ** -0.5),
        "o_router_w": nrm((N_ODD, D, N_EXPERTS), D ** -0.5),
        "o_router_b": nrm((N_ODD, N_EXPERTS), 0.01),
        "o_moe_wg": nrm((N_ODD, N_EXPERTS, D, EXPERT_DIM), D ** -0.5),
        "o_moe_wu": nrm((N_ODD, N_EXPERTS, D, EXPERT_DIM), D ** -0.5),
        "o_moe_wd": nrm((N_ODD, N_EXPERTS, EXPERT_DIM, D), EXPERT_DIM ** -0.5),
    }


def reference(x, c, e_ada_w, e_ada_b, e_norm_mix, e_norm_ffn, e_w_in, e_conv_a_w, e_conv_a_b,
              e_ln_a_g, e_ln_a_b, e_conv_b_w, e_conv_b_b, e_lru_wa, e_lru_ba, e_lru_wx, e_lru_bx,
              e_lru_lambda, e_w_out, e_ffn_wg, e_ffn_wu, e_ffn_wd,
              o_ada_w, o_ada_b, o_norm_mix, o_norm_ffn, o_w_in, o_conv_c_w, o_q_norm, o_k_norm,
              o_w_out, o_router_w, o_router_b, o_moe_wg, o_moe_wu, o_moe_wd):
    pos = jnp.arange(x.shape[1])
    c_act = jax.nn.silu(c)
    for layer in range(DEPTH):
        j = layer // 2
        if layer % 2 == 0:
            mod = (c_act @ e_ada_w[j] + e_ada_b[j])[:, None, :]
            sh1, sc1, g1, sh2, sc2, g2 = jnp.split(mod, 6, axis=-1)
            h = rms_norm(x, e_norm_mix[j]) * (1.0 + sc1) + sh1
            x = x + g1 * mixer_conformer_rglru(
                h, e_w_in[j], e_conv_a_w[j], e_conv_a_b[j], e_ln_a_g[j], e_ln_a_b[j],
                e_conv_b_w[j], e_conv_b_b[j], e_lru_wa[j], e_lru_ba[j], e_lru_wx[j], e_lru_bx[j],
                e_lru_lambda[j], e_w_out[j])
            h = rms_norm(x, e_norm_ffn[j]) * (1.0 + sc2) + sh2
            x = x + g2 * swiglu(h, e_ffn_wg[j], e_ffn_wu[j], e_ffn_wd[j])
        else:
            mod = (c_act @ o_ada_w[j] + o_ada_b[j])[:, None, :]
            sh1, sc1, g1, sh2, sc2, g2 = jnp.split(mod, 6, axis=-1)
            h = rms_norm(x, o_norm_mix[j]) * (1.0 + sc1) + sh1
            x = x + g1 * mixer_shortconv_moba(h, o_w_in[j], o_conv_c_w[j], o_q_norm[j], o_k_norm[j],
                                              o_w_out[j], pos)
            h = rms_norm(x, o_norm_ffn[j]) * (1.0 + sc2) + sh2
            x = x + g2 * moe_swiglu(h, o_router_w[j], o_router_b[j], o_moe_wg[j], o_moe_wu[j],
                                    o_moe_wd[j])
    return x
```

```python
import functools

import jax
import jax.numpy as jnp
from jax import lax
from jax.experimental import pallas as pl
from jax.experimental.pallas import tpu as pltpu

F32 = jnp.float32
BF16 = jnp.bfloat16

N_GROUPS = 8
GROUP_DIM = 64
MIX_W = N_GROUPS * GROUP_DIM
CONF_WIDTH = 31
LRU_CONV_WIDTH = 4
LRU_C = 8.0
SHORT_CONV_WIDTH = 3
MOBA_BLOCK = 256
MOBA_TOPK = 3
ROPE_THETA = 10000.0
N_EXPERTS = 8
EPS = 1e-6
DEPTH = 2

LANES = 128
SUBLANES = 8
VMEM_LIMIT_BYTES = 56 * 1024 * 1024

MIX_ROWS = 512
FFN_ROWS = 1024
FFN_COLS = 1408
MOE_ROWS = 2048
MOE_COLS = 512
MOE_CHUNK = 256
CONV_A_HALO = 32
CONV_S_HALO = 8


def _dot(a, b):
    return jnp.dot(a, b, preferred_element_type=F32)


def _dot_nt(a, b):
    return lax.dot_general(a, b, (((1,), (1,)), ((), ())), preferred_element_type=F32)


def _sigmoid(x):
    return 1.0 / (1.0 + jnp.exp(-x))


def _rms_mod(x, gain, scale, shift):
    ms = jnp.mean(x * x, axis=-1, keepdims=True)
    return (x * lax.rsqrt(ms + EPS)) * gain * (1.0 + scale) + shift


def _params(*sem):
    return pltpu.CompilerParams(dimension_semantics=sem, vmem_limit_bytes=VMEM_LIMIT_BYTES)


def _full(shape):
    n = len(shape)
    return pl.BlockSpec(shape, lambda *_: (0,) * n)


def _mod_kernel(c_ref, w_ref, b_ref, o_ref):
    c = c_ref[...]
    c_act = c * _sigmoid(c)
    o_ref[...] = _dot(c_act.astype(BF16), w_ref[...].astype(BF16)) + b_ref[...]


def _modulation(c_pad, w, b):
    d, n = w.shape
    tn = n // 4
    return pl.pallas_call(
        _mod_kernel,
        grid=(4,),
        in_specs=[_full(c_pad.shape),
                  pl.BlockSpec((d, tn), lambda j: (0, j)),
                  pl.BlockSpec((1, tn), lambda j: (0, j))],
        out_specs=pl.BlockSpec((c_pad.shape[0], tn), lambda j: (0, j)),
        out_shape=jax.ShapeDtypeStruct((c_pad.shape[0], n), F32),
        compiler_params=_params("arbitrary"),
        name="adaln_mod",
    )(c_pad, w, b)


def _causal_conv(buf, halo, rows, w_ref, width):
    acc = None
    for j in range(width):
        start = halo - (width - 1) + j
        term = w_ref[j:j + 1, :] * buf[start:start + rows, :]
        acc = term if acc is None else acc + term
    return acc


def _even_mixer_kernel(x_ref, mod_ref, gain_ref, win_ref, caw_ref, cab_ref, lng_ref, lnb_ref,
                       cbw_ref, cbb_ref, wa_ref, ba_ref, wx_ref, bx_ref, lam_ref, wout_ref,
                       o_ref, abuf, bbuf, hcar):
    rows = x_ref.shape[1]

    @pl.when(pl.program_id(1) == 0)
    def _():
        abuf[0:CONV_A_HALO, :] = jnp.zeros((CONV_A_HALO, MIX_W), F32)
        bbuf[0:CONV_S_HALO, :] = jnp.zeros((CONV_S_HALO, MIX_W), F32)
        hcar[...] = jnp.zeros(hcar.shape, F32)

    x = x_ref[0]
    h = _rms_mod(x, gain_ref[...], mod_ref[0, 1:2, :], mod_ref[0, 0:1, :])
    u = _dot(h.astype(BF16), win_ref[...])
    a_val = u[:, 0 * MIX_W:1 * MIX_W]
    a_gate = u[:, 1 * MIX_W:2 * MIX_W]
    b_x = u[:, 2 * MIX_W:3 * MIX_W]
    b_gate = u[:, 3 * MIX_W:4 * MIX_W]

    abuf[CONV_A_HALO:CONV_A_HALO + rows, :] = a_val * _sigmoid(a_gate)
    ya = _causal_conv(abuf, CONV_A_HALO, rows, caw_ref, CONF_WIDTH) + cab_ref[...]
    abuf[0:CONV_A_HALO, :] = abuf[rows:rows + CONV_A_HALO, :]
    mu = jnp.mean(ya, axis=-1, keepdims=True)
    dev = ya - mu
    var = jnp.mean(dev * dev, axis=-1, keepdims=True)
    ya = dev * lax.rsqrt(var + EPS) * lng_ref[...] + lnb_ref[...]
    ya = ya * _sigmoid(ya)

    bbuf[CONV_S_HALO:CONV_S_HALO + rows, :] = b_x
    xb = _causal_conv(bbuf, CONV_S_HALO, rows, cbw_ref, LRU_CONV_WIDTH) + cbb_ref[...]
    bbuf[0:CONV_S_HALO, :] = bbuf[rows:rows + CONV_S_HALO, :]
    xb16 = xb.astype(BF16)
    r_gate = _sigmoid(_dot(xb16, wa_ref[...]) + ba_ref[...])
    i_gate = _sigmoid(_dot(xb16, wx_ref[...]) + bx_ref[...])
    lam = lam_ref[...]
    log_sig = -(jnp.maximum(-lam, 0.0) + jnp.log1p(jnp.exp(-jnp.abs(lam))))
    log_a = LRU_C * r_gate * log_sig
    a = jnp.exp(log_a)
    mult = jnp.sqrt(-jnp.tanh(log_a) * (1.0 + a * a))
    bterm = mult * (i_gate * xb)

    row = lax.broadcasted_iota(jnp.int32, (rows, MIX_W), 0)
    d = 1
    while d < rows:
        keep = row >= d
        a_prev = jnp.where(keep, pltpu.roll(a, d, 0), 1.0)
        b_prev = jnp.where(keep, pltpu.roll(bterm, d, 0), 0.0)
        bterm = bterm + a * b_prev
        a = a * a_prev
        d *= 2
    hseq = bterm + a * hcar[0:1, :]
    hcar[...] = jnp.broadcast_to(hseq[rows - 1:rows, :], hcar.shape)

    gelu = 0.5 * b_gate * (1.0 + jnp.tanh(0.7978845608028654 * (b_gate + 0.044715 * (b_gate * b_gate * b_gate))))
    yb = hseq * gelu

    mix = _dot(ya.astype(BF16), wout_ref[0:MIX_W, :]) + _dot(yb.astype(BF16), wout_ref[MIX_W:2 * MIX_W, :])
    o_ref[0] = x + mod_ref[0, 2:3, :] * mix


def _even_mixer(x, mod, gain, w_in, caw, cab, lng, lnb, cbw, cbb, wa, ba, wx, bx, lam, w_out):
    b, s, d = x.shape
    rows = min(MIX_ROWS, s)
    row_spec = pl.BlockSpec((1, rows, d), lambda i, j: (i, j, 0))
    small = [gain, w_in, caw, cab, lng, lnb, cbw, cbb, wa, ba, wx, bx, lam, w_out]
    return pl.pallas_call(
        _even_mixer_kernel,
        grid=(b, s // rows),
        in_specs=[row_spec, pl.BlockSpec((1, 6, d), lambda i, j: (i, 0, 0))] + [_full(t.shape) for t in small],
        out_specs=row_spec,
        out_shape=jax.ShapeDtypeStruct(x.shape, F32),
        scratch_shapes=[pltpu.VMEM((CONV_A_HALO + rows, MIX_W), F32),
                        pltpu.VMEM((CONV_S_HALO + rows, MIX_W), F32),
                        pltpu.VMEM((SUBLANES, MIX_W), F32)],
        compiler_params=_params("arbitrary", "arbitrary"),
        name="even_mixer",
    )(x, mod, *small)


def _ffn_kernel(x_ref, mod_ref, gain_ref, wg_ref, wu_ref, wd_ref, o_ref, h_scr):
    @pl.when(pl.program_id(2) == 0)
    def _():
        x = x_ref[0]
        h_scr[...] = _rms_mod(x, gain_ref[...], mod_ref[0, 4:5, :], mod_ref[0, 3:4, :]).astype(BF16)
        o_ref[0] = x

    h = h_scr[...]
    g = _dot(h, wg_ref[...])
    act = (g * _sigmoid(g)) * _dot(h, wu_ref[...])
    o_ref[0] += mod_ref[0, 5:6, :] * _dot(act.astype(BF16), wd_ref[...])


def _dense_ffn(x, mod, gain, wg, wu, wd):
    b, s, d = x.shape
    f = wg.shape[1]
    rows = min(FFN_ROWS, s)
    cols = FFN_COLS if f % FFN_COLS == 0 else f
    row_spec = pl.BlockSpec((1, rows, d), lambda i, j, k: (i, j, 0))
    return pl.pallas_call(
        _ffn_kernel,
        grid=(b, s // rows, f // cols),
        in_specs=[row_spec,
                  pl.BlockSpec((1, 6, d), lambda i, j, k: (i, 0, 0)),
                  _full(gain.shape),
                  pl.BlockSpec((d, cols), lambda i, j, k: (0, k)),
                  pl.BlockSpec((d, cols), lambda i, j, k: (0, k)),
                  pl.BlockSpec((cols, d), lambda i, j, k: (k, 0))],
        out_specs=row_spec,
        out_shape=jax.ShapeDtypeStruct(x.shape, F32),
        scratch_shapes=[pltpu.VMEM((rows, d), BF16)],
        compiler_params=_params("arbitrary", "arbitrary", "arbitrary"),
        name="dense_ffn",
    )(x, mod, gain, wg, wu, wd)


def _odd_mixer_kernel(x_ref, mod_ref, gain_ref, win_ref, ccw_ref, qn_ref, kn_ref, seg_ref,
                      cos_ref, sa_ref, sb_ref, yc_ref, q_ref, k_ref, v_ref, km_ref, cbuf):
    rows = x_ref.shape[1]

    @pl.when(pl.program_id(1) == 0)
    def _():
        cbuf[0:CONV_S_HALO, :] = jnp.zeros((CONV_S_HALO, MIX_W), F32)

    x = x_ref[0]
    h = _rms_mod(x, gain_ref[...], mod_ref[0, 1:2, :], mod_ref[0, 0:1, :])
    u = _dot(h.astype(BF16), win_ref[...])
    c_h = u[:, 0 * MIX_W:1 * MIX_W]
    c_b = u[:, 1 * MIX_W:2 * MIX_W]
    c_c = u[:, 2 * MIX_W:3 * MIX_W]
    q = u[:, 3 * MIX_W:4 * MIX_W]
    k = u[:, 4 * MIX_W:5 * MIX_W]
    v = u[:, 5 * MIX_W:6 * MIX_W]

    cbuf[CONV_S_HALO:CONV_S_HALO + rows, :] = c_c * c_h
    conv = _causal_conv(cbuf, CONV_S_HALO, rows, ccw_ref, SHORT_CONV_WIDTH)
    cbuf[0:CONV_S_HALO, :] = cbuf[rows:rows + CONV_S_HALO, :]
    yc_ref[0] = (c_b * conv).astype(yc_ref.dtype)

    seg = seg_ref[...]
    cos = cos_ref[...]
    sin_a = sa_ref[...]
    sin_b = sb_ref[...]

    def head_norm_rope(t, gain):
        t2 = t * t
        hi = t2.astype(BF16)
        lo = (t2 - hi.astype(F32)).astype(BF16)
        ssq = _dot(hi, seg) + _dot(lo, seg)
        t = t * lax.rsqrt(ssq * (1.0 / GROUP_DIM) + EPS) * gain
        outs = []
        for g in range(MIX_W // LANES):
            tg = t[:, g * LANES:(g + 1) * LANES]
            outs.append(tg * cos + pltpu.roll(tg, LANES - GROUP_DIM // 2, 1) * sin_a
                        + pltpu.roll(tg, GROUP_DIM // 2, 1) * sin_b)
        return jnp.concatenate(outs, axis=1)

    qr = head_norm_rope(q, qn_ref[...])
    kr = head_norm_rope(k, kn_ref[...])
    q_ref[0] = (qr * (GROUP_DIM ** -0.5)).astype(q_ref.dtype)
    k_ref[0] = kr.astype(k_ref.dtype)
    v_ref[0] = v.astype(v_ref.dtype)
    for i in range(rows // MOBA_BLOCK):
        km_ref[0, i] = jnp.mean(kr[i * MOBA_BLOCK:(i + 1) * MOBA_BLOCK, :], axis=0, keepdims=True)


def _odd_mixer(x, mod, gain, w_in, ccw, qn, kn, seg, cos, sin_a, sin_b):
    b, s, d = x.shape
    rows = min(MIX_ROWS, s)
    nb = s // MOBA_BLOCK
    row_spec = pl.BlockSpec((1, rows, d), lambda i, j: (i, j, 0))
    mix_spec = pl.BlockSpec((1, rows, MIX_W), lambda i, j: (i, j, 0))
    tab_spec = pl.BlockSpec((rows, LANES), lambda i, j: (j, 0))
    small = [gain, w_in, ccw, qn, kn, seg]
    mix_shape = jax.ShapeDtypeStruct((b, s, MIX_W), BF16)
    return pl.pallas_call(
        _odd_mixer_kernel,
        grid=(b, s // rows),
        in_specs=[row_spec, pl.BlockSpec((1, 6, d), lambda i, j: (i, 0, 0))]
                 + [_full(t.shape) for t in small] + [tab_spec, tab_spec, tab_spec],
        out_specs=[mix_spec, mix_spec, mix_spec, mix_spec,
                   pl.BlockSpec((1, rows // MOBA_BLOCK, 1, MIX_W), lambda i, j: (i, j, 0, 0))],
        out_shape=[mix_shape, mix_shape, mix_shape, mix_shape,
                   jax.ShapeDtypeStruct((b, nb, 1, MIX_W), F32)],
        scratch_shapes=[pltpu.VMEM((CONV_S_HALO + rows, MIX_W), F32)],
        compiler_params=_params("arbitrary", "arbitrary"),
        name="odd_mixer",
    )(x, mod, *small, cos, sin_a, sin_b)


def _top_blocks(gate, blk, own):
    remaining = blk < own
    sel = jnp.zeros(gate.shape, F32)
    nblk = gate.shape[1]
    for _ in range(MOBA_TOPK):
        gm = jnp.where(remaining, gate, -jnp.inf)
        top = jnp.max(gm, axis=1, keepdims=True)
        cand = remaining & (gm == top)
        first = jnp.min(jnp.where(cand, blk, nblk), axis=1, keepdims=True)
        pick = cand & (blk == first)
        sel = jnp.where(pick, 1.0, sel)
        remaining = remaining & jnp.logical_not(pick)
    return sel


def _moba_kernel(q_ref, k_ref, v_ref, km_ref, o_ref):
    own = pl.program_id(2)
    nblk = km_ref.shape[1]
    q = q_ref[0]
    lane = lax.broadcasted_iota(jnp.int32, q.shape, 1)
    kmean = km_ref[0, :, 0, :].astype(BF16)
    row = lax.broadcasted_iota(jnp.int32, (MOBA_BLOCK, MOBA_BLOCK), 0)
    col = lax.broadcasted_iota(jnp.int32, (MOBA_BLOCK, MOBA_BLOCK), 1)
    causal = col <= row
    blk = lax.broadcasted_iota(jnp.int32, (MOBA_BLOCK, nblk), 1)

    def kv_block(j):
        start = pl.multiple_of(j * MOBA_BLOCK, MOBA_BLOCK)
        return k_ref[0, pl.ds(start, MOBA_BLOCK), :], v_ref[0, pl.ds(start, MOBA_BLOCK), :]

    k_own, v_own = kv_block(own)
    heads = []
    for hh in range(LANES // GROUP_DIM):
        in_head = (lane >= hh * GROUP_DIM) & (lane < (hh + 1) * GROUP_DIM)
        qh = jnp.where(in_head, q, jnp.zeros_like(q))
        sel = _top_blocks(_dot_nt(qh, kmean), blk, own)

        s = jnp.where(causal, _dot_nt(qh, k_own), -jnp.inf)
        m = jnp.max(s, axis=1, keepdims=True)
        p = jnp.exp(s - m)
        l = jnp.sum(p, axis=1, keepdims=True)
        acc = _dot(p.astype(BF16), v_own)

        def body(j, carry):
            m, l, acc = carry
            kj, vj = kv_block(j)
            chosen = jnp.sum(jnp.where(blk == j, sel, 0.0), axis=1, keepdims=True)
            s = jnp.where(chosen > 0.0, _dot_nt(qh, kj), -jnp.inf)
            m_new = jnp.maximum(m, jnp.max(s, axis=1, keepdims=True))
            alpha = jnp.exp(m - m_new)
            p = jnp.exp(s - m_new)
            l = alpha * l + jnp.sum(p, axis=1, keepdims=True)
            acc = alpha * acc + _dot(p.astype(BF16), vj)
            return m_new, l, acc

        m, l, acc = lax.fori_loop(0, own, body, (m, l, acc))
        heads.append(acc / l)
    o_ref[0] = jnp.where(lane < GROUP_DIM, heads[0], heads[1]).astype(o_ref.dtype)


def _moba(q, k, v, kmean):
    b, s, _ = q.shape
    nb = s // MOBA_BLOCK
    q_spec = pl.BlockSpec((1, MOBA_BLOCK, LANES), lambda i, p, j: (i, j, p))
    kv_spec = pl.BlockSpec((1, s, LANES), lambda i, p, j: (i, 0, p))
    return pl.pallas_call(
        _moba_kernel,
        grid=(b, MIX_W // LANES, nb),
        in_specs=[q_spec, kv_spec, kv_spec,
                  pl.BlockSpec((1, nb, 1, LANES), lambda i, p, j: (i, 0, 0, p))],
        out_specs=q_spec,
        out_shape=jax.ShapeDtypeStruct(q.shape, BF16),
        compiler_params=_params("arbitrary", "arbitrary", "arbitrary"),
        name="moba_attention",
    )(q, k, v, kmean)


def _odd_out_kernel(yc_ref, yd_ref, x_ref, mod_ref, wout_ref, gain_ref, rwh_ref, rwl_ref, rb_ref,
                    x2_ref, h2_ref, wt_ref, mk_ref, cnt_ref):
    x = x_ref[0]
    mix = _dot(yc_ref[0], wout_ref[0:MIX_W, :]) + _dot(yd_ref[0], wout_ref[MIX_W:2 * MIX_W, :])
    x2 = x + mod_ref[0, 2:3, :] * mix
    x2_ref[0] = x2
    h = _rms_mod(x2, gain_ref[...], mod_ref[0, 4:5, :], mod_ref[0, 3:4, :])
    h_hi = h.astype(BF16)
    h2_ref[0] = h_hi.astype(h2_ref.dtype)

    h_lo = (h - h_hi.astype(F32)).astype(BF16)
    rwh = rwh_ref[...]
    logits = _dot(h_hi, rwh) + _dot(h_lo, rwh) + _dot(h_hi, rwl_ref[...]) + rb_ref[...]
    lane = lax.broadcasted_iota(jnp.int32, logits.shape, 1)
    valid = lane < N_EXPERTS
    lg = jnp.where(valid, logits, -jnp.inf)
    top1 = jnp.max(lg, axis=1, keepdims=True)
    idx1 = jnp.min(jnp.where(lg == top1, lane, LANES), axis=1, keepdims=True)
    pick1 = lane == idx1
    lg2 = jnp.where(pick1, -jnp.inf, lg)
    top2 = jnp.max(lg2, axis=1, keepdims=True)
    idx2 = jnp.min(jnp.where((lg2 == top2) & valid & jnp.logical_not(pick1), lane, LANES), axis=1, keepdims=True)
    pick2 = lane == idx2
    e2 = jnp.exp(top2 - top1)
    den = 1.0 + e2
    wt_ref[0] = jnp.where(pick1, 1.0 / den, 0.0) + jnp.where(pick2, e2 / den, 0.0)
    mask = jnp.where(pick1 | pick2, 1.0, 0.0)
    mk_ref[0] = mask.astype(mk_ref.dtype)
    cnt_ref[0, 0] = jnp.sum(mask, axis=0, keepdims=True)


def _odd_out(yc, yd, x, mod, w_out, gain, rw_hi, rw_lo, rb):
    b, s, d = x.shape
    rows = min(MIX_ROWS, s)
    row_spec = pl.BlockSpec((1, rows, d), lambda i, j: (i, j, 0))
    mix_spec = pl.BlockSpec((1, rows, MIX_W), lambda i, j: (i, j, 0))
    lane_spec = pl.BlockSpec((1, rows, LANES), lambda i, j: (i, j, 0))
    small = [w_out, gain, rw_hi, rw_lo, rb]
    return pl.pallas_call(
        _odd_out_kernel,
        grid=(b, s // rows),
        in_specs=[mix_spec, mix_spec, row_spec, pl.BlockSpec((1, 6, d), lambda i, j: (i, 0, 0))]
                 + [_full(t.shape) for t in small],
        out_specs=[row_spec, row_spec, lane_spec, lane_spec,
                   pl.BlockSpec((1, 1, 1, LANES), lambda i, j: (i, j, 0, 0))],
        out_shape=[jax.ShapeDtypeStruct(x.shape, F32),
                   jax.ShapeDtypeStruct(x.shape, BF16),
                   jax.ShapeDtypeStruct((b, s, LANES), F32),
                   jax.ShapeDtypeStruct((b, s, LANES), BF16),
                   jax.ShapeDtypeStruct((b, s // rows, 1, LANES), F32)],
        compiler_params=_params("arbitrary", "arbitrary"),
        name="odd_out_router",
    )(yc, yd, x, mod, *small)


def _moe_kernel(nch_ref, h2_ref, wt_ref, mk_ref, x2_ref, mod_ref, wg_ref, wu_ref, wd_ref, o_ref,
                posc_scr, posr_scr, xs_scr, acc_scr):
    tile = pl.program_id(0)
    e = pl.program_id(1)
    f = pl.program_id(2)
    rows = h2_ref.shape[0]
    n_chunks = nch_ref[tile * N_EXPERTS + e]
    ch = MOE_CHUNK

    @pl.when((e == 0) & (f == 0))
    def _positions():
        o_ref[...] = jnp.zeros(o_ref.shape, F32)
        r_i = lax.broadcasted_iota(jnp.int32, (ch, ch), 0)
        c_i = lax.broadcasted_iota(jnp.int32, (ch, ch), 1)
        lower = jnp.where(c_i < r_i, 1.0, 0.0).astype(BF16)
        upper = jnp.where(r_i < c_i, 1.0, 0.0).astype(BF16)
        eye = jnp.where(lax.broadcasted_iota(jnp.int32, (N_EXPERTS, LANES), 0)
                        == lax.broadcasted_iota(jnp.int32, (N_EXPERTS, LANES), 1), 1.0, 0.0).astype(BF16)
        carry_c = jnp.zeros((1, LANES), F32)
        carry_r = jnp.zeros((N_EXPERTS, 1), F32)
        for blk in range(rows // ch):
            sl = slice(blk * ch, (blk + 1) * ch)
            mb = mk_ref[sl, :]
            mbf = mb.astype(F32)
            posc_scr[sl, :] = jnp.where(mbf > 0.0, _dot(lower, mb) + carry_c, -1.0)
            carry_c = carry_c + jnp.sum(mbf, axis=0, keepdims=True)
            mbt = _dot_nt(eye, mb)
            posr_scr[:, sl] = jnp.where(mbt > 0.0, _dot(mbt.astype(BF16), upper) + carry_r, -1.0)
            carry_r = carry_r + jnp.sum(mbt, axis=1, keepdims=True)

    @pl.when(f == 0)
    def _gather():
        posr = posr_scr[pl.ds(e, 1), :]
        r_i = lax.broadcasted_iota(jnp.int32, (ch, rows), 0).astype(F32)

        def body(kk, carry):
            off = pl.multiple_of(kk * ch, ch)
            onehot = jnp.where(posr - (kk * ch).astype(F32) == r_i, 1.0, 0.0).astype(BF16)
            xs_scr[pl.ds(off, ch), :] = _dot(onehot, h2_ref[...]).astype(xs_scr.dtype)
            acc_scr[pl.ds(off, ch), :] = jnp.zeros((ch, acc_scr.shape[1]), F32)
            return carry

        lax.fori_loop(0, n_chunks, body, 0)

    def ffn_body(kk, carry):
        off = pl.multiple_of(kk * ch, ch)
        xk = xs_scr[pl.ds(off, ch), :]
        g = _dot(xk, wg_ref[0])
        act = (g * _sigmoid(g)) * _dot(xk, wu_ref[0])
        acc_scr[pl.ds(off, ch), :] += _dot(act.astype(BF16), wd_ref[0])
        return carry

    lax.fori_loop(0, n_chunks, ffn_body, 0)

    @pl.when(f == pl.num_programs(2) - 1)
    def _scatter():
        sub = 512
        lane = lax.broadcasted_iota(jnp.int32, (sub, LANES), 1)
        c_i = lax.broadcasted_iota(jnp.int32, (sub, ch), 1).astype(F32)

        def body(kk, carry):
            off = pl.multiple_of(kk * ch, ch)
            y = acc_scr[pl.ds(off, ch), :].astype(BF16)
            base = (kk * ch).astype(F32)
            for tb in range(rows // sub):
                sl = slice(tb * sub, (tb + 1) * sub)
                posc = jnp.sum(jnp.where(lane == e, posc_scr[sl, :], 0.0), axis=1, keepdims=True)
                wcol = jnp.sum(jnp.where(lane == e, wt_ref[sl, :], 0.0), axis=1, keepdims=True)
                onehot = jnp.where(posc - base == c_i, 1.0, 0.0).astype(BF16)
                o_ref[sl, :] += wcol * _dot(onehot, y)
            return carry

        lax.fori_loop(0, n_chunks, body, 0)

        @pl.when(e == N_EXPERTS - 1)
        def _residual():
            o_ref[...] = x2_ref[...] + mod_ref[0, 5:6, :] * o_ref[...]


def _moe(h2, wts, mask, x2, mod, n_chunks, wg, wu, wd, seq):
    t, d = h2.shape
    fdim = wg.shape[2]
    rows = min(MOE_ROWS, seq)
    cols = MOE_COLS if fdim % MOE_COLS == 0 else fdim
    once = pl.Buffered(1)
    tile_spec = lambda width: pl.BlockSpec((rows, width), lambda i, e, f, n: (i, 0), pipeline_mode=once)
    grid_spec = pltpu.PrefetchScalarGridSpec(
        num_scalar_prefetch=1,
        grid=(t // rows, N_EXPERTS, fdim // cols),
        in_specs=[tile_spec(d), tile_spec(LANES), tile_spec(LANES), tile_spec(d),
                  pl.BlockSpec((1, 6, d), lambda i, e, f, n: (i * rows // seq, 0, 0)),
                  pl.BlockSpec((1, d, cols), lambda i, e, f, n: (e, 0, f)),
                  pl.BlockSpec((1, d, cols), lambda i, e, f, n: (e, 0, f)),
                  pl.BlockSpec((1, cols, d), lambda i, e, f, n: (e, f, 0))],
        out_specs=pl.BlockSpec((rows, d), lambda i, e, f, n: (i, 0)),
        scratch_shapes=[pltpu.VMEM((rows, LANES), F32),
                        pltpu.VMEM((N_EXPERTS, rows), F32),
                        pltpu.VMEM((rows, d), BF16),
                        pltpu.VMEM((rows, d), F32)],
    )
    return pl.pallas_call(
        _moe_kernel,
        grid_spec=grid_spec,
        out_shape=jax.ShapeDtypeStruct((t, d), F32),
        compiler_params=_params("arbitrary", "arbitrary", "arbitrary"),
        name="moe_experts",
    )(n_chunks, h2, wts, mask, x2, mod, wg, wu, wd)


def _block_diag(w):
    h, i, j = w.shape
    eye = jnp.eye(h, dtype=w.dtype)
    return jnp.einsum('hij,hk->hikj', w, eye).reshape(h * i, h * j)


def _rope_tables(seq):
    half = GROUP_DIM // 2
    inv = ROPE_THETA ** (-jnp.arange(half, dtype=F32) / half)
    ang = jnp.arange(seq).astype(F32)[:, None] * inv[None, :]
    cos, sin, zero = jnp.cos(ang), jnp.sin(ang), jnp.zeros_like(ang)
    reps = LANES // GROUP_DIM
    return (jnp.tile(jnp.concatenate([cos, cos], axis=1), (1, reps)),
            jnp.tile(jnp.concatenate([-sin, zero], axis=1), (1, reps)),
            jnp.tile(jnp.concatenate([zero, sin], axis=1), (1, reps)))


def _row(v):
    return v.reshape(1, -1).astype(F32)


def kernel(x, c, e_ada_w, e_ada_b, e_norm_mix, e_norm_ffn, e_w_in, e_conv_a_w, e_conv_a_b,
           e_ln_a_g, e_ln_a_b, e_conv_b_w, e_conv_b_b, e_lru_wa, e_lru_ba, e_lru_wx, e_lru_bx,
           e_lru_lambda, e_w_out, e_ffn_wg, e_ffn_wu, e_ffn_wd,
           o_ada_w, o_ada_b, o_norm_mix, o_norm_ffn, o_w_in, o_conv_c_w, o_q_norm, o_k_norm,
           o_w_out, o_router_w, o_router_b, o_moe_wg, o_moe_wu, o_moe_wd):
    b, s, d = x.shape
    assert s % MOBA_BLOCK == 0 and d % LANES == 0
    c_pad = jnp.zeros((SUBLANES, d), F32).at[:b].set(c.astype(F32))
    seg = jnp.kron(jnp.eye(N_GROUPS, dtype=F32), jnp.ones((GROUP_DIM, GROUP_DIM), F32)).astype(BF16)
    cos, sin_a, sin_b = _rope_tables(s)

    def modulation(w, bias):
        return _modulation(c_pad, w, _row(bias))[:b].reshape(b, 6, d)

    x = x.astype(F32)
    for layer in range(DEPTH):
        j = layer // 2
        if layer % 2 == 0:
            mod = modulation(e_ada_w[j], e_ada_b[j])
            x = _even_mixer(
                x, mod, _row(e_norm_mix[j]), e_w_in[j].astype(BF16),
                e_conv_a_w[j].astype(F32), _row(e_conv_a_b[j]), _row(e_ln_a_g[j]), _row(e_ln_a_b[j]),
                e_conv_b_w[j].astype(F32), _row(e_conv_b_b[j]),
                _block_diag(e_lru_wa[j]).astype(BF16), _row(e_lru_ba[j]),
                _block_diag(e_lru_wx[j]).astype(BF16), _row(e_lru_bx[j]),
                _row(e_lru_lambda[j]), e_w_out[j].astype(BF16))
            x = _dense_ffn(x, mod, _row(e_norm_ffn[j]), e_ffn_wg[j].astype(BF16),
                           e_ffn_wu[j].astype(BF16), e_ffn_wd[j].astype(BF16))
        else:
            mod = modulation(o_ada_w[j], o_ada_b[j])
            yc, q, k, v, kmean = _odd_mixer(
                x, mod, _row(o_norm_mix[j]), o_w_in[j].astype(BF16), o_conv_c_w[j].astype(F32),
                _row(jnp.tile(o_q_norm[j], N_GROUPS)), _row(jnp.tile(o_k_norm[j], N_GROUPS)),
                seg, cos, sin_a, sin_b)
            yd = _moba(q, k, v, kmean)
            rw = jnp.zeros((d, LANES), F32).at[:, :N_EXPERTS].set(o_router_w[j].astype(F32))
            rw_hi = rw.astype(BF16)
            rw_lo = (rw - rw_hi.astype(F32)).astype(BF16)
            rb = jnp.zeros((1, LANES), F32).at[0, :N_EXPERTS].set(o_router_b[j].astype(F32))
            x2, h2, wts, mask, cnt = _odd_out(yc, yd, x, mod, o_w_out[j].astype(BF16),
                                              _row(o_norm_ffn[j]), rw_hi, rw_lo, rb)
            rows = min(MOE_ROWS, s)
            per_tile = cnt.reshape(b * s // rows, -1, LANES)[:, :, :N_EXPERTS].sum(axis=1)
            n_chunks = ((per_tile.astype(jnp.int32) + MOE_CHUNK - 1) // MOE_CHUNK).reshape(-1)
            out = _moe(h2.reshape(b * s, d), wts.reshape(b * s, LANES), mask.reshape(b * s, LANES),
                       x2.reshape(b * s, d), mod, n_chunks,
                       o_moe_wg[j].astype(BF16), o_moe_wu[j].astype(BF16), o_moe_wd[j].astype(BF16), s)
            x = out.reshape(b, s, d)
    return x
```

```python
import functools

import jax
import jax.numpy as jnp
from jax import lax
from jax.experimental import pallas as pl
from jax.experimental.pallas import tpu as pltpu

F32 = jnp.float32
BF16 = jnp.bfloat16

N_GROUPS = 8
GROUP_DIM = 64
MIX_W = N_GROUPS * GROUP_DIM
CONF_WIDTH = 31
LRU_CONV_WIDTH = 4
LRU_C = 8.0
SHORT_CONV_WIDTH = 3
MOBA_BLOCK = 256
MOBA_TOPK = 3
ROPE_THETA = 10000.0
N_EXPERTS = 8
EPS = 1e-6
DEPTH = 2

LANES = 128
SUBLANES = 8
VMEM_LIMIT_BYTES = 56 * 1024 * 1024

MIX_ROWS = 512
FFN_ROWS = 1024
FFN_COLS = 1408
MOE_ROWS = 2048
MOE_COLS = 512
MOE_CHUNK = 256
MOBA_KEY_GROUP = 4
CONV_A_HALO = 32
CONV_S_HALO = 8


def _dot(a, b):
    return jnp.dot(a, b, preferred_element_type=F32)


def _dot_nt(a, b):
    return lax.dot_general(a, b, (((1,), (1,)), ((), ())), preferred_element_type=F32)


def _sigmoid(x):
    return 1.0 / (1.0 + jnp.exp(-x))


def _rms_mod(x, gain, scale, shift):
    ms = jnp.mean(x * x, axis=-1, keepdims=True)
    return (x * lax.rsqrt(ms + EPS)) * gain * (1.0 + scale) + shift


def _params(*sem):
    return pltpu.CompilerParams(dimension_semantics=sem, vmem_limit_bytes=VMEM_LIMIT_BYTES)


def _full(shape):
    n = len(shape)
    return pl.BlockSpec(shape, lambda *_: (0,) * n)


def _mod_kernel(c_ref, w_ref, b_ref, o_ref):
    c = c_ref[...]
    c_act = c * _sigmoid(c)
    o_ref[...] = _dot(c_act.astype(BF16), w_ref[...].astype(BF16)) + b_ref[...]


def _modulation(c_pad, w, b):
    d, n = w.shape
    tn = n // 4
    return pl.pallas_call(
        _mod_kernel,
        grid=(4,),
        in_specs=[_full(c_pad.shape),
                  pl.BlockSpec((d, tn), lambda j: (0, j)),
                  pl.BlockSpec((1, tn), lambda j: (0, j))],
        out_specs=pl.BlockSpec((c_pad.shape[0], tn), lambda j: (0, j)),
        out_shape=jax.ShapeDtypeStruct((c_pad.shape[0], n), F32),
        compiler_params=_params("arbitrary"),
        name="adaln_mod",
    )(c_pad, w, b)


def _causal_conv(buf, halo, rows, w_ref, width):
    acc = None
    for j in range(width):
        start = halo - (width - 1) + j
        term = w_ref[j:j + 1, :] * buf[start:start + rows, :]
        acc = term if acc is None else acc + term
    return acc


def _even_mixer_kernel(x_ref, mod_ref, gain_ref, win_ref, caw_ref, cab_ref, lng_ref, lnb_ref,
                       cbw_ref, cbb_ref, wa_ref, ba_ref, wx_ref, bx_ref, lam_ref, wout_ref,
                       o_ref, abuf, bbuf, hcar):
    rows = x_ref.shape[1]

    @pl.when(pl.program_id(1) == 0)
    def _():
        abuf[0:CONV_A_HALO, :] = jnp.zeros((CONV_A_HALO, MIX_W), F32)
        bbuf[0:CONV_S_HALO, :] = jnp.zeros((CONV_S_HALO, MIX_W), F32)
        hcar[...] = jnp.zeros(hcar.shape, F32)

    x = x_ref[0]
    h = _rms_mod(x, gain_ref[...], mod_ref[0, 1:2, :], mod_ref[0, 0:1, :])
    u = _dot(h.astype(BF16), win_ref[...])
    a_val = u[:, 0 * MIX_W:1 * MIX_W]
    a_gate = u[:, 1 * MIX_W:2 * MIX_W]
    b_x = u[:, 2 * MIX_W:3 * MIX_W]
    b_gate = u[:, 3 * MIX_W:4 * MIX_W]

    abuf[CONV_A_HALO:CONV_A_HALO + rows, :] = a_val * _sigmoid(a_gate)
    ya = _causal_conv(abuf, CONV_A_HALO, rows, caw_ref, CONF_WIDTH) + cab_ref[...]
    abuf[0:CONV_A_HALO, :] = abuf[rows:rows + CONV_A_HALO, :]
    mu = jnp.mean(ya, axis=-1, keepdims=True)
    dev = ya - mu
    var = jnp.mean(dev * dev, axis=-1, keepdims=True)
    ya = dev * lax.rsqrt(var + EPS) * lng_ref[...] + lnb_ref[...]
    ya = ya * _sigmoid(ya)

    bbuf[CONV_S_HALO:CONV_S_HALO + rows, :] = b_x
    xb = _causal_conv(bbuf, CONV_S_HALO, rows, cbw_ref, LRU_CONV_WIDTH) + cbb_ref[...]
    bbuf[0:CONV_S_HALO, :] = bbuf[rows:rows + CONV_S_HALO, :]
    xb16 = xb.astype(BF16)
    r_gate = _sigmoid(_dot(xb16, wa_ref[...]) + ba_ref[...])
    i_gate = _sigmoid(_dot(xb16, wx_ref[...]) + bx_ref[...])
    lam = lam_ref[...]
    log_sig = -(jnp.maximum(-lam, 0.0) + jnp.log1p(jnp.exp(-jnp.abs(lam))))
    log_a = LRU_C * r_gate * log_sig
    a = jnp.exp(log_a)
    mult = jnp.sqrt(-jnp.tanh(log_a) * (1.0 + a * a))
    bterm = mult * (i_gate * xb)

    row = lax.broadcasted_iota(jnp.int32, (rows, MIX_W), 0)
    d = 1
    while d < rows:
        keep = row >= d
        a_prev = jnp.where(keep, pltpu.roll(a, d, 0), 1.0)
        b_prev = jnp.where(keep, pltpu.roll(bterm, d, 0), 0.0)
        bterm = bterm + a * b_prev
        a = a * a_prev
        d *= 2
    hseq = bterm + a * hcar[0:1, :]
    hcar[...] = jnp.broadcast_to(hseq[rows - 1:rows, :], hcar.shape)

    gelu = 0.5 * b_gate * (1.0 + jnp.tanh(0.7978845608028654 * (b_gate + 0.044715 * (b_gate * b_gate * b_gate))))
    yb = hseq * gelu

    mix = _dot(ya.astype(BF16), wout_ref[0:MIX_W, :]) + _dot(yb.astype(BF16), wout_ref[MIX_W:2 * MIX_W, :])
    o_ref[0] = x + mod_ref[0, 2:3, :] * mix


def _even_mixer(x, mod, gain, w_in, caw, cab, lng, lnb, cbw, cbb, wa, ba, wx, bx, lam, w_out):
    b, s, d = x.shape
    rows = min(MIX_ROWS, s)
    row_spec = pl.BlockSpec((1, rows, d), lambda i, j: (i, j, 0))
    small = [gain, w_in, caw, cab, lng, lnb, cbw, cbb, wa, ba, wx, bx, lam, w_out]
    return pl.pallas_call(
        _even_mixer_kernel,
        grid=(b, s // rows),
        in_specs=[row_spec, pl.BlockSpec((1, 6, d), lambda i, j: (i, 0, 0))] + [_full(t.shape) for t in small],
        out_specs=row_spec,
        out_shape=jax.ShapeDtypeStruct(x.shape, F32),
        scratch_shapes=[pltpu.VMEM((CONV_A_HALO + rows, MIX_W), F32),
                        pltpu.VMEM((CONV_S_HALO + rows, MIX_W), F32),
                        pltpu.VMEM((SUBLANES, MIX_W), F32)],
        compiler_params=_params("arbitrary", "arbitrary"),
        name="even_mixer",
    )(x, mod, *small)


def _ffn_kernel(x_ref, mod_ref, gain_ref, wg_ref, wu_ref, wd_ref, o_ref, h_scr):
    @pl.when(pl.program_id(2) == 0)
    def _():
        x = x_ref[0]
        h_scr[...] = _rms_mod(x, gain_ref[...], mod_ref[0, 4:5, :], mod_ref[0, 3:4, :]).astype(BF16)
        o_ref[0] = x

    h = h_scr[...]
    g = _dot(h, wg_ref[...])
    act = (g * _sigmoid(g)) * _dot(h, wu_ref[...])
    o_ref[0] += mod_ref[0, 5:6, :] * _dot(act.astype(BF16), wd_ref[...])


def _dense_ffn(x, mod, gain, wg, wu, wd):
    b, s, d = x.shape
    f = wg.shape[1]
    rows = min(FFN_ROWS, s)
    cols = FFN_COLS if f % FFN_COLS == 0 else f
    row_spec = pl.BlockSpec((1, rows, d), lambda i, j, k: (i, j, 0))
    return pl.pallas_call(
        _ffn_kernel,
        grid=(b, s // rows, f // cols),
        in_specs=[row_spec,
                  pl.BlockSpec((1, 6, d), lambda i, j, k: (i, 0, 0)),
                  _full(gain.shape),
                  pl.BlockSpec((d, cols), lambda i, j, k: (0, k)),
                  pl.BlockSpec((d, cols), lambda i, j, k: (0, k)),
                  pl.BlockSpec((cols, d), lambda i, j, k: (k, 0))],
        out_specs=row_spec,
        out_shape=jax.ShapeDtypeStruct(x.shape, F32),
        scratch_shapes=[pltpu.VMEM((rows, d), BF16)],
        compiler_params=_params("arbitrary", "arbitrary", "arbitrary"),
        name="dense_ffn",
    )(x, mod, gain, wg, wu, wd)


def _odd_mixer_kernel(x_ref, mod_ref, gain_ref, win_ref, ccw_ref, qn_ref, kn_ref, seg_ref,
                      cos_ref, sa_ref, sb_ref, yc_ref, q_ref, k_ref, v_ref, km_ref, cbuf):
    rows = x_ref.shape[1]

    @pl.when(pl.program_id(1) == 0)
    def _():
        cbuf[0:CONV_S_HALO, :] = jnp.zeros((CONV_S_HALO, MIX_W), F32)

    x = x_ref[0]
    h = _rms_mod(x, gain_ref[...], mod_ref[0, 1:2, :], mod_ref[0, 0:1, :])
    u = _dot(h.astype(BF16), win_ref[...])
    c_h = u[:, 0 * MIX_W:1 * MIX_W]
    c_b = u[:, 1 * MIX_W:2 * MIX_W]
    c_c = u[:, 2 * MIX_W:3 * MIX_W]
    q = u[:, 3 * MIX_W:4 * MIX_W]
    k = u[:, 4 * MIX_W:5 * MIX_W]
    v = u[:, 5 * MIX_W:6 * MIX_W]

    cbuf[CONV_S_HALO:CONV_S_HALO + rows, :] = c_c * c_h
    conv = _causal_conv(cbuf, CONV_S_HALO, rows, ccw_ref, SHORT_CONV_WIDTH)
    cbuf[0:CONV_S_HALO, :] = cbuf[rows:rows + CONV_S_HALO, :]
    yc_ref[0] = (c_b * conv).astype(yc_ref.dtype)

    seg = seg_ref[...]
    cos = cos_ref[...]
    sin_a = sa_ref[...]
    sin_b = sb_ref[...]

    def head_norm_rope(t, gain):
        t2 = t * t
        hi = t2.astype(BF16)
        lo = (t2 - hi.astype(F32)).astype(BF16)
        ssq = _dot(hi, seg) + _dot(lo, seg)
        t = t * lax.rsqrt(ssq * (1.0 / GROUP_DIM) + EPS) * gain
        outs = []
        for g in range(MIX_W // LANES):
            tg = t[:, g * LANES:(g + 1) * LANES]
            outs.append(tg * cos + pltpu.roll(tg, LANES - GROUP_DIM // 2, 1) * sin_a
                        + pltpu.roll(tg, GROUP_DIM // 2, 1) * sin_b)
        return jnp.concatenate(outs, axis=1)

    qr = head_norm_rope(q, qn_ref[...])
    kr = head_norm_rope(k, kn_ref[...])
    q_ref[0] = (qr * (GROUP_DIM ** -0.5)).astype(q_ref.dtype)
    k_ref[0] = kr.astype(k_ref.dtype)
    v_ref[0] = v.astype(v_ref.dtype)
    for i in range(rows // MOBA_BLOCK):
        km_ref[0, i] = jnp.mean(kr[i * MOBA_BLOCK:(i + 1) * MOBA_BLOCK, :], axis=0, keepdims=True)


def _odd_mixer(x, mod, gain, w_in, ccw, qn, kn, seg, cos, sin_a, sin_b):
    b, s, d = x.shape
    rows = min(MIX_ROWS, s)
    nb = s // MOBA_BLOCK
    row_spec = pl.BlockSpec((1, rows, d), lambda i, j: (i, j, 0))
    mix_spec = pl.BlockSpec((1, rows, MIX_W), lambda i, j: (i, j, 0))
    tab_spec = pl.BlockSpec((rows, LANES), lambda i, j: (j, 0))
    small = [gain, w_in, ccw, qn, kn, seg]
    mix_shape = jax.ShapeDtypeStruct((b, s, MIX_W), BF16)
    return pl.pallas_call(
        _odd_mixer_kernel,
        grid=(b, s // rows),
        in_specs=[row_spec, pl.BlockSpec((1, 6, d), lambda i, j: (i, 0, 0))]
                 + [_full(t.shape) for t in small] + [tab_spec, tab_spec, tab_spec],
        out_specs=[mix_spec, mix_spec, mix_spec, mix_spec,
                   pl.BlockSpec((1, rows // MOBA_BLOCK, 1, MIX_W), lambda i, j: (i, j, 0, 0))],
        out_shape=[mix_shape, mix_shape, mix_shape, mix_shape,
                   jax.ShapeDtypeStruct((b, nb, 1, MIX_W), F32)],
        scratch_shapes=[pltpu.VMEM((CONV_S_HALO + rows, MIX_W), F32)],
        compiler_params=_params("arbitrary", "arbitrary"),
        name="odd_mixer",
    )(x, mod, *small, cos, sin_a, sin_b)


def _top_blocks(gate, own):
    nblk = gate.shape[0]
    blk = lax.broadcasted_iota(jnp.int32, gate.shape, 0)
    remaining = blk < own
    sel = jnp.zeros(gate.shape, F32)
    for _ in range(MOBA_TOPK):
        gm = jnp.where(remaining, gate, -jnp.inf)
        top = jnp.max(gm, axis=0, keepdims=True)
        cand = remaining & (gm == top)
        first = jnp.min(jnp.where(cand, blk, nblk), axis=0, keepdims=True)
        pick = cand & (blk == first)
        sel = jnp.where(pick, 1.0, sel)
        remaining = remaining & jnp.logical_not(pick)
    return sel


def _moba_kernel(q_ref, k_ref, v_ref, km_ref, o_ref):
    own = pl.program_id(2)
    nblk = km_ref.shape[1]
    blk_rows = MOBA_BLOCK
    q = q_ref[0]
    lane = lax.broadcasted_iota(jnp.int32, q.shape, 1)
    zero = jnp.zeros_like(q)
    kmean = km_ref[0, :, 0, :].astype(BF16)
    row = lax.broadcasted_iota(jnp.int32, (blk_rows, blk_rows), 0)
    col = lax.broadcasted_iota(jnp.int32, (blk_rows, blk_rows), 1)
    causal_bias = jnp.where(col <= row, 0.0, -jnp.inf)
    pad = jnp.zeros((LANES - nblk, blk_rows), F32)

    q_heads, bias_heads = [], []
    for hh in range(LANES // GROUP_DIM):
        in_head = (lane >= hh * GROUP_DIM) & (lane < (hh + 1) * GROUP_DIM)
        qh = jnp.where(in_head, q, zero)
        sel_t = _top_blocks(_dot_nt(kmean, qh), own)
        sel = jnp.transpose(jnp.concatenate([sel_t, pad], axis=0))
        q_heads.append(qh)
        bias_heads.append(jnp.where(sel > 0.0, 0.0, -jnp.inf))

    group = MOBA_KEY_GROUP
    for g in range(-(-nblk // group)):
        nk = min((g + 1) * group, nblk)

        @pl.when(own // group == g)
        def _(g=g, nk=nk):
            keys = k_ref[0, 0:nk * blk_rows, :]
            vals = v_ref[0, 0:nk * blk_rows, :]
            outs = []
            for qh, bias in zip(q_heads, bias_heads):
                s = _dot_nt(qh, keys)
                pieces = []
                for jb in range(nk):
                    blk_bias = bias[:, jb:jb + 1]
                    if jb >= g * group:
                        blk_bias = jnp.where(own == jb, causal_bias, blk_bias)
                    pieces.append(s[:, jb * blk_rows:(jb + 1) * blk_rows] + blk_bias)
                s = jnp.concatenate(pieces, axis=1)
                m = jnp.max(s, axis=1, keepdims=True)
                p = jnp.exp(s - m)
                l = jnp.sum(p, axis=1, keepdims=True)
                outs.append(_dot(p.astype(BF16), vals) / l)
            o_ref[0] = jnp.where(lane < GROUP_DIM, outs[0], outs[1]).astype(o_ref.dtype)


def _moba(q, k, v, kmean):
    b, s, _ = q.shape
    nb = s // MOBA_BLOCK
    q_spec = pl.BlockSpec((1, MOBA_BLOCK, LANES), lambda i, p, j: (i, j, p))
    kv_spec = pl.BlockSpec((1, s, LANES), lambda i, p, j: (i, 0, p))
    return pl.pallas_call(
        _moba_kernel,
        grid=(b, MIX_W // LANES, nb),
        in_specs=[q_spec, kv_spec, kv_spec,
                  pl.BlockSpec((1, nb, 1, LANES), lambda i, p, j: (i, 0, 0, p))],
        out_specs=q_spec,
        out_shape=jax.ShapeDtypeStruct(q.shape, BF16),
        compiler_params=_params("arbitrary", "arbitrary", "arbitrary"),
        name="moba_attention",
    )(q, k, v, kmean)


def _odd_out_kernel(yc_ref, yd_ref, x_ref, mod_ref, wout_ref, gain_ref, rwh_ref, rwl_ref, rb_ref,
                    x2_ref, h2_ref, wt_ref, mk_ref, cnt_ref):
    x = x_ref[0]
    mix = _dot(yc_ref[0], wout_ref[0:MIX_W, :]) + _dot(yd_ref[0], wout_ref[MIX_W:2 * MIX_W, :])
    x2 = x + mod_ref[0, 2:3, :] * mix
    x2_ref[0] = x2
    h = _rms_mod(x2, gain_ref[...], mod_ref[0, 4:5, :], mod_ref[0, 3:4, :])
    h_hi = h.astype(BF16)
    h2_ref[0] = h_hi.astype(h2_ref.dtype)

    h_lo = (h - h_hi.astype(F32)).astype(BF16)
    rwh = rwh_ref[...]
    logits = _dot(h_hi, rwh) + _dot(h_lo, rwh) + _dot(h_hi, rwl_ref[...]) + rb_ref[...]
    lane = lax.broadcasted_iota(jnp.int32, logits.shape, 1)
    valid = lane < N_EXPERTS
    lg = jnp.where(valid, logits, -jnp.inf)
    top1 = jnp.max(lg, axis=1, keepdims=True)
    idx1 = jnp.min(jnp.where(lg == top1, lane, LANES), axis=1, keepdims=True)
    pick1 = lane == idx1
    lg2 = jnp.where(pick1, -jnp.inf, lg)
    top2 = jnp.max(lg2, axis=1, keepdims=True)
    idx2 = jnp.min(jnp.where((lg2 == top2) & valid & jnp.logical_not(pick1), lane, LANES), axis=1, keepdims=True)
    pick2 = lane == idx2
    e2 = jnp.exp(top2 - top1)
    den = 1.0 + e2
    wt_ref[0] = jnp.where(pick1, 1.0 / den, 0.0) + jnp.where(pick2, e2 / den, 0.0)
    mask = jnp.where(pick1 | pick2, 1.0, 0.0)
    mk_ref[0] = mask.astype(mk_ref.dtype)
    cnt_ref[0, 0] = jnp.sum(mask, axis=0, keepdims=True)


def _odd_out(yc, yd, x, mod, w_out, gain, rw_hi, rw_lo, rb):
    b, s, d = x.shape
    rows = min(MIX_ROWS, s)
    row_spec = pl.BlockSpec((1, rows, d), lambda i, j: (i, j, 0))
    mix_spec = pl.BlockSpec((1, rows, MIX_W), lambda i, j: (i, j, 0))
    lane_spec = pl.BlockSpec((1, rows, LANES), lambda i, j: (i, j, 0))
    small = [w_out, gain, rw_hi, rw_lo, rb]
    return pl.pallas_call(
        _odd_out_kernel,
        grid=(b, s // rows),
        in_specs=[mix_spec, mix_spec, row_spec, pl.BlockSpec((1, 6, d), lambda i, j: (i, 0, 0))]
                 + [_full(t.shape) for t in small],
        out_specs=[row_spec, row_spec, lane_spec, lane_spec,
                   pl.BlockSpec((1, 1, 1, LANES), lambda i, j: (i, j, 0, 0))],
        out_shape=[jax.ShapeDtypeStruct(x.shape, F32),
                   jax.ShapeDtypeStruct(x.shape, BF16),
                   jax.ShapeDtypeStruct((b, s, LANES), F32),
                   jax.ShapeDtypeStruct((b, s, LANES), BF16),
                   jax.ShapeDtypeStruct((b, s // rows, 1, LANES), F32)],
        compiler_params=_params("arbitrary", "arbitrary"),
        name="odd_out_router",
    )(yc, yd, x, mod, *small)


def _moe_kernel(nch_ref, h2_ref, wt_ref, mk_ref, x2_ref, mod_ref, wg_ref, wu_ref, wd_ref, o_ref,
                posc_scr, posr_scr, xs_scr, acc_scr):
    tile = pl.program_id(0)
    e = pl.program_id(1)
    f = pl.program_id(2)
    rows = h2_ref.shape[0]
    n_chunks = nch_ref[tile * N_EXPERTS + e]
    ch = MOE_CHUNK

    @pl.when((e == 0) & (f == 0))
    def _positions():
        o_ref[...] = jnp.zeros(o_ref.shape, F32)
        r_i = lax.broadcasted_iota(jnp.int32, (ch, ch), 0)
        c_i = lax.broadcasted_iota(jnp.int32, (ch, ch), 1)
        lower = jnp.where(c_i < r_i, 1.0, 0.0).astype(BF16)
        upper = jnp.where(r_i < c_i, 1.0, 0.0).astype(BF16)
        eye = jnp.where(lax.broadcasted_iota(jnp.int32, (N_EXPERTS, LANES), 0)
                        == lax.broadcasted_iota(jnp.int32, (N_EXPERTS, LANES), 1), 1.0, 0.0).astype(BF16)
        carry_c = jnp.zeros((1, LANES), F32)
        carry_r = jnp.zeros((N_EXPERTS, 1), F32)
        for blk in range(rows // ch):
            sl = slice(blk * ch, (blk + 1) * ch)
            mb = mk_ref[sl, :]
            mbf = mb.astype(F32)
            posc_scr[sl, :] = jnp.where(mbf > 0.0, _dot(lower, mb) + carry_c, -1.0)
            carry_c = carry_c + jnp.sum(mbf, axis=0, keepdims=True)
            mbt = _dot_nt(eye, mb)
            posr_scr[:, sl] = jnp.where(mbt > 0.0, _dot(mbt.astype(BF16), upper) + carry_r, -1.0)
            carry_r = carry_r + jnp.sum(mbt, axis=1, keepdims=True)

    @pl.when(f == 0)
    def _gather():
        posr = posr_scr[pl.ds(e, 1), :]
        r_i = lax.broadcasted_iota(jnp.int32, (ch, rows), 0).astype(F32)

        def body(kk, carry):
            off = pl.multiple_of(kk * ch, ch)
            onehot = jnp.where(posr - (kk * ch).astype(F32) == r_i, 1.0, 0.0).astype(BF16)
            xs_scr[pl.ds(off, ch), :] = _dot(onehot, h2_ref[...]).astype(xs_scr.dtype)
            acc_scr[pl.ds(off, ch), :] = jnp.zeros((ch, acc_scr.shape[1]), F32)
            return carry

        lax.fori_loop(0, n_chunks, body, 0)

    def ffn_body(kk, carry):
        off = pl.multiple_of(kk * ch, ch)
        xk = xs_scr[pl.ds(off, ch), :]
        g = _dot(xk, wg_ref[0])
        act = (g * _sigmoid(g)) * _dot(xk, wu_ref[0])
        acc_scr[pl.ds(off, ch), :] += _dot(act.astype(BF16), wd_ref[0])
        return carry

    lax.fori_loop(0, n_chunks, ffn_body, 0)

    @pl.when(f == pl.num_programs(2) - 1)
    def _scatter():
        sub = 512
        lane = lax.broadcasted_iota(jnp.int32, (sub, LANES), 1)
        c_i = lax.broadcasted_iota(jnp.int32, (sub, ch), 1).astype(F32)

        def body(kk, carry):
            off = pl.multiple_of(kk * ch, ch)
            y = acc_scr[pl.ds(off, ch), :].astype(BF16)
            base = (kk * ch).astype(F32)
            for tb in range(rows // sub):
                sl = slice(tb * sub, (tb + 1) * sub)
                posc = jnp.sum(jnp.where(lane == e, posc_scr[sl, :], 0.0), axis=1, keepdims=True)
                wcol = jnp.sum(jnp.where(lane == e, wt_ref[sl, :], 0.0), axis=1, keepdims=True)
                onehot = jnp.where(posc - base == c_i, 1.0, 0.0).astype(BF16)
                o_ref[sl, :] += wcol * _dot(onehot, y)
            return carry

        lax.fori_loop(0, n_chunks, body, 0)

        @pl.when(e == N_EXPERTS - 1)
        def _residual():
            o_ref[...] = x2_ref[...] + mod_ref[0, 5:6, :] * o_ref[...]


def _moe(h2, wts, mask, x2, mod, n_chunks, wg, wu, wd, seq):
    t, d = h2.shape
    fdim = wg.shape[2]
    rows = min(MOE_ROWS, seq)
    cols = MOE_COLS if fdim % MOE_COLS == 0 else fdim
    once = pl.Buffered(1)
    tile_spec = lambda width: pl.BlockSpec((rows, width), lambda i, e, f, n: (i, 0), pipeline_mode=once)
    grid_spec = pltpu.PrefetchScalarGridSpec(
        num_scalar_prefetch=1,
        grid=(t // rows, N_EXPERTS, fdim // cols),
        in_specs=[tile_spec(d), tile_spec(LANES), tile_spec(LANES), tile_spec(d),
                  pl.BlockSpec((1, 6, d), lambda i, e, f, n: (i * rows // seq, 0, 0)),
                  pl.BlockSpec((1, d, cols), lambda i, e, f, n: (e, 0, f)),
                  pl.BlockSpec((1, d, cols), lambda i, e, f, n: (e, 0, f)),
                  pl.BlockSpec((1, cols, d), lambda i, e, f, n: (e, f, 0))],
        out_specs=pl.BlockSpec((rows, d), lambda i, e, f, n: (i, 0)),
        scratch_shapes=[pltpu.VMEM((rows, LANES), F32),
                        pltpu.VMEM((N_EXPERTS, rows), F32),
                        pltpu.VMEM((rows, d), BF16),
                        pltpu.VMEM((rows, d), F32)],
    )
    return pl.pallas_call(
        _moe_kernel,
        grid_spec=grid_spec,
        out_shape=jax.ShapeDtypeStruct((t, d), F32),
        compiler_params=_params("arbitrary", "arbitrary", "arbitrary"),
        name="moe_experts",
    )(n_chunks, h2, wts, mask, x2, mod, wg, wu, wd)


def _block_diag(w):
    h, i, j = w.shape
    eye = jnp.eye(h, dtype=w.dtype)
    return jnp.einsum('hij,hk->hikj', w, eye).reshape(h * i, h * j)


def _rope_tables(seq):
    half = GROUP_DIM // 2
    inv = ROPE_THETA ** (-jnp.arange(half, dtype=F32) / half)
    ang = jnp.arange(seq).astype(F32)[:, None] * inv[None, :]
    cos, sin, zero = jnp.cos(ang), jnp.sin(ang), jnp.zeros_like(ang)
    reps = LANES // GROUP_DIM
    return (jnp.tile(jnp.concatenate([cos, cos], axis=1), (1, reps)),
            jnp.tile(jnp.concatenate([-sin, zero], axis=1), (1, reps)),
            jnp.tile(jnp.concatenate([zero, sin], axis=1), (1, reps)))


def _row(v):
    return v.reshape(1, -1).astype(F32)


def kernel(x, c, e_ada_w, e_ada_b, e_norm_mix, e_norm_ffn, e_w_in, e_conv_a_w, e_conv_a_b,
           e_ln_a_g, e_ln_a_b, e_conv_b_w, e_conv_b_b, e_lru_wa, e_lru_ba, e_lru_wx, e_lru_bx,
           e_lru_lambda, e_w_out, e_ffn_wg, e_ffn_wu, e_ffn_wd,
           o_ada_w, o_ada_b, o_norm_mix, o_norm_ffn, o_w_in, o_conv_c_w, o_q_norm, o_k_norm,
           o_w_out, o_router_w, o_router_b, o_moe_wg, o_moe_wu, o_moe_wd):
    b, s, d = x.shape
    assert s % MOBA_BLOCK == 0 and d % LANES == 0
    c_pad = jnp.zeros((SUBLANES, d), F32).at[:b].set(c.astype(F32))
    seg = jnp.kron(jnp.eye(N_GROUPS, dtype=F32), jnp.ones((GROUP_DIM, GROUP_DIM), F32)).astype(BF16)
    cos, sin_a, sin_b = _rope_tables(s)

    def modulation(w, bias):
        return _modulation(c_pad, w, _row(bias))[:b].reshape(b, 6, d)

    x = x.astype(F32)
    for layer in range(DEPTH):
        j = layer // 2
        if layer % 2 == 0:
            mod = modulation(e_ada_w[j], e_ada_b[j])
            x = _even_mixer(
                x, mod, _row(e_norm_mix[j]), e_w_in[j].astype(BF16),
                e_conv_a_w[j].astype(F32), _row(e_conv_a_b[j]), _row(e_ln_a_g[j]), _row(e_ln_a_b[j]),
                e_conv_b_w[j].astype(F32), _row(e_conv_b_b[j]),
                _block_diag(e_lru_wa[j]).astype(BF16), _row(e_lru_ba[j]),
                _block_diag(e_lru_wx[j]).astype(BF16), _row(e_lru_bx[j]),
                _row(e_lru_lambda[j]), e_w_out[j].astype(BF16))
            x = _dense_ffn(x, mod, _row(e_norm_ffn[j]), e_ffn_wg[j].astype(BF16),
                           e_ffn_wu[j].astype(BF16), e_ffn_wd[j].astype(BF16))
        else:
            mod = modulation(o_ada_w[j], o_ada_b[j])
            yc, q, k, v, kmean = _odd_mixer(
                x, mod, _row(o_norm_mix[j]), o_w_in[j].astype(BF16), o_conv_c_w[j].astype(F32),
                _row(jnp.tile(o_q_norm[j], N_GROUPS)), _row(jnp.tile(o_k_norm[j], N_GROUPS)),
                seg, cos, sin_a, sin_b)
            yd = _moba(q, k, v, kmean)
            rw = jnp.zeros((d, LANES), F32).at[:, :N_EXPERTS].set(o_router_w[j].astype(F32))
            rw_hi = rw.astype(BF16)
            rw_lo = (rw - rw_hi.astype(F32)).astype(BF16)
            rb = jnp.zeros((1, LANES), F32).at[0, :N_EXPERTS].set(o_router_b[j].astype(F32))
            x2, h2, wts, mask, cnt = _odd_out(yc, yd, x, mod, o_w_out[j].astype(BF16),
                                              _row(o_norm_ffn[j]), rw_hi, rw_lo, rb)
            rows = min(MOE_ROWS, s)
            per_tile = cnt.reshape(b * s // rows, -1, LANES)[:, :, :N_EXPERTS].sum(axis=1)
            n_chunks = ((per_tile.astype(jnp.int32) + MOE_CHUNK - 1) // MOE_CHUNK).reshape(-1)
            out = _moe(h2.reshape(b * s, d), wts.reshape(b * s, LANES), mask.reshape(b * s, LANES),
                       x2.reshape(b * s, d), mod, n_chunks,
                       o_moe_wg[j].astype(BF16), o_moe_wu[j].astype(BF16), o_moe_wd[j].astype(BF16), s)
            x = out.reshape(b, s, d)
    return x
```

```python
import functools

import jax
import jax.numpy as jnp
from jax import lax
from jax.experimental import pallas as pl
from jax.experimental.pallas import tpu as pltpu

F32 = jnp.float32
BF16 = jnp.bfloat16

N_GROUPS = 8
GROUP_DIM = 64
MIX_W = N_GROUPS * GROUP_DIM
CONF_WIDTH = 31
LRU_CONV_WIDTH = 4
LRU_C = 8.0
SHORT_CONV_WIDTH = 3
MOBA_BLOCK = 256
MOBA_TOPK = 3
ROPE_THETA = 10000.0
N_EXPERTS = 8
EPS = 1e-6
DEPTH = 2

LANES = 128
SUBLANES = 8
VMEM_LIMIT_BYTES = 56 * 1024 * 1024

MIX_ROWS = 512
FFN_ROWS = 1024
FFN_COLS = 1408
MOE_ROWS = 2048
MOE_COLS = 512
MOE_CHUNK = 256
MOE_BIG_ROWS = 512
MOE_SMALL_ROWS = 128
MOE_SCATTER_ROWS = 512
MOBA_KEY_GROUP = 4
CONV_A_HALO = 32
CONV_S_HALO = 8


def _dot(a, b):
    return jnp.dot(a, b, preferred_element_type=F32)


def _dot_nt(a, b):
    return lax.dot_general(a, b, (((1,), (1,)), ((), ())), preferred_element_type=F32)


def _sigmoid(x):
    return 1.0 / (1.0 + jnp.exp(-x))


def _rms_mod(x, gain, scale, shift):
    ms = jnp.mean(x * x, axis=-1, keepdims=True)
    return (x * lax.rsqrt(ms + EPS)) * gain * (1.0 + scale) + shift


def _params(*sem):
    return pltpu.CompilerParams(dimension_semantics=sem, vmem_limit_bytes=VMEM_LIMIT_BYTES)


def _full(shape):
    n = len(shape)
    return pl.BlockSpec(shape, lambda *_: (0,) * n)


def _mod_kernel(c_ref, w_ref, b_ref, o_ref):
    c = c_ref[...]
    c_act = c * _sigmoid(c)
    o_ref[...] = _dot(c_act.astype(BF16), w_ref[...].astype(BF16)) + b_ref[...]


def _modulation(c_pad, w, b):
    d, n = w.shape
    tn = n // 4
    return pl.pallas_call(
        _mod_kernel,
        grid=(4,),
        in_specs=[_full(c_pad.shape),
                  pl.BlockSpec((d, tn), lambda j: (0, j)),
                  pl.BlockSpec((1, tn), lambda j: (0, j))],
        out_specs=pl.BlockSpec((c_pad.shape[0], tn), lambda j: (0, j)),
        out_shape=jax.ShapeDtypeStruct((c_pad.shape[0], n), F32),
        compiler_params=_params("arbitrary"),
        name="adaln_mod",
    )(c_pad, w, b)


def _causal_conv(buf, halo, rows, w_ref, width):
    total = halo + rows
    full = buf[0:total, :]
    acc = None
    for phase in range(SUBLANES):
        taps = [j for j in range(width) if (halo - (width - 1) + j) % SUBLANES == phase]
        if not taps:
            continue
        shifted = full if phase == 0 else pltpu.roll(full, total - phase, 0)
        for j in taps:
            start = halo - (width - 1) + j - phase
            term = w_ref[j:j + 1, :] * shifted[start:start + rows, :]
            acc = term if acc is None else acc + term
    return acc


def _even_mixer_kernel(x_ref, mod_ref, gain_ref, win_ref, caw_ref, cab_ref, lng_ref, lnb_ref,
                       cbw_ref, cbb_ref, wa_ref, ba_ref, wx_ref, bx_ref, lam_ref, wout_ref,
                       o_ref, abuf, bbuf, hcar):
    rows = x_ref.shape[1]

    @pl.when(pl.program_id(1) == 0)
    def _():
        abuf[0:CONV_A_HALO, :] = jnp.zeros((CONV_A_HALO, MIX_W), F32)
        bbuf[0:CONV_S_HALO, :] = jnp.zeros((CONV_S_HALO, MIX_W), F32)
        hcar[...] = jnp.zeros(hcar.shape, F32)

    x = x_ref[0]
    h = _rms_mod(x, gain_ref[...], mod_ref[0, 1:2, :], mod_ref[0, 0:1, :])
    u = _dot(h.astype(BF16), win_ref[...])
    a_val = u[:, 0 * MIX_W:1 * MIX_W]
    a_gate = u[:, 1 * MIX_W:2 * MIX_W]
    b_x = u[:, 2 * MIX_W:3 * MIX_W]
    b_gate = u[:, 3 * MIX_W:4 * MIX_W]

    abuf[CONV_A_HALO:CONV_A_HALO + rows, :] = a_val * _sigmoid(a_gate)
    ya = _causal_conv(abuf, CONV_A_HALO, rows, caw_ref, CONF_WIDTH) + cab_ref[...]
    abuf[0:CONV_A_HALO, :] = abuf[rows:rows + CONV_A_HALO, :]
    mu = jnp.mean(ya, axis=-1, keepdims=True)
    dev = ya - mu
    var = jnp.mean(dev * dev, axis=-1, keepdims=True)
    ya = dev * lax.rsqrt(var + EPS) * lng_ref[...] + lnb_ref[...]
    ya = ya * _sigmoid(ya)

    bbuf[CONV_S_HALO:CONV_S_HALO + rows, :] = b_x
    xb = _causal_conv(bbuf, CONV_S_HALO, rows, cbw_ref, LRU_CONV_WIDTH) + cbb_ref[...]
    bbuf[0:CONV_S_HALO, :] = bbuf[rows:rows + CONV_S_HALO, :]
    xb16 = xb.astype(BF16)
    r_gate = _sigmoid(_dot(xb16, wa_ref[...]) + ba_ref[...])
    i_gate = _sigmoid(_dot(xb16, wx_ref[...]) + bx_ref[...])
    lam = lam_ref[...]
    log_sig = -(jnp.maximum(-lam, 0.0) + jnp.log1p(jnp.exp(-jnp.abs(lam))))
    log_a = LRU_C * r_gate * log_sig
    a = jnp.exp(log_a)
    mult = jnp.sqrt(-jnp.tanh(log_a) * (1.0 + a * a))
    bterm = mult * (i_gate * xb)

    row = lax.broadcasted_iota(jnp.int32, (rows, MIX_W), 0)
    d = 1
    while d < rows:
        keep = row >= d
        a_prev = jnp.where(keep, pltpu.roll(a, d, 0), 1.0)
        b_prev = jnp.where(keep, pltpu.roll(bterm, d, 0), 0.0)
        bterm = bterm + a * b_prev
        a = a * a_prev
        d *= 2
    hseq = bterm + a * hcar[0:1, :]
    hcar[...] = jnp.broadcast_to(hseq[rows - 1:rows, :], hcar.shape)

    gelu = 0.5 * b_gate * (1.0 + jnp.tanh(0.7978845608028654 * (b_gate + 0.044715 * (b_gate * b_gate * b_gate))))
    yb = hseq * gelu

    mix = _dot(ya.astype(BF16), wout_ref[0:MIX_W, :]) + _dot(yb.astype(BF16), wout_ref[MIX_W:2 * MIX_W, :])
    o_ref[0] = x + mod_ref[0, 2:3, :] * mix


def _even_mixer(x, mod, gain, w_in, caw, cab, lng, lnb, cbw, cbb, wa, ba, wx, bx, lam, w_out):
    b, s, d = x.shape
    rows = min(MIX_ROWS, s)
    row_spec = pl.BlockSpec((1, rows, d), lambda i, j: (i, j, 0))
    small = [gain, w_in, caw, cab, lng, lnb, cbw, cbb, wa, ba, wx, bx, lam, w_out]
    return pl.pallas_call(
        _even_mixer_kernel,
        grid=(b, s // rows),
        in_specs=[row_spec, pl.BlockSpec((1, 6, d), lambda i, j: (i, 0, 0))] + [_full(t.shape) for t in small],
        out_specs=row_spec,
        out_shape=jax.ShapeDtypeStruct(x.shape, F32),
        scratch_shapes=[pltpu.VMEM((CONV_A_HALO + rows, MIX_W), F32),
                        pltpu.VMEM((CONV_S_HALO + rows, MIX_W), F32),
                        pltpu.VMEM((SUBLANES, MIX_W), F32)],
        compiler_params=_params("arbitrary", "arbitrary"),
        name="even_mixer",
    )(x, mod, *small)


def _ffn_kernel(x_ref, mod_ref, gain_ref, wg_ref, wu_ref, wd_ref, o_ref, h_scr):
    @pl.when(pl.program_id(2) == 0)
    def _():
        x = x_ref[0]
        h_scr[...] = _rms_mod(x, gain_ref[...], mod_ref[0, 4:5, :], mod_ref[0, 3:4, :]).astype(BF16)
        o_ref[0] = x

    h = h_scr[...]
    g = _dot(h, wg_ref[...])
    act = (g * _sigmoid(g)) * _dot(h, wu_ref[...])
    o_ref[0] += mod_ref[0, 5:6, :] * _dot(act.astype(BF16), wd_ref[...])


def _dense_ffn(x, mod, gain, wg, wu, wd):
    b, s, d = x.shape
    f = wg.shape[1]
    rows = min(FFN_ROWS, s)
    cols = FFN_COLS if f % FFN_COLS == 0 else f
    row_spec = pl.BlockSpec((1, rows, d), lambda i, j, k: (i, j, 0))
    return pl.pallas_call(
        _ffn_kernel,
        grid=(b, s // rows, f // cols),
        in_specs=[row_spec,
                  pl.BlockSpec((1, 6, d), lambda i, j, k: (i, 0, 0)),
                  _full(gain.shape),
                  pl.BlockSpec((d, cols), lambda i, j, k: (0, k)),
                  pl.BlockSpec((d, cols), lambda i, j, k: (0, k)),
                  pl.BlockSpec((cols, d), lambda i, j, k: (k, 0))],
        out_specs=row_spec,
        out_shape=jax.ShapeDtypeStruct(x.shape, F32),
        scratch_shapes=[pltpu.VMEM((rows, d), BF16)],
        compiler_params=_params("arbitrary", "arbitrary", "arbitrary"),
        name="dense_ffn",
    )(x, mod, gain, wg, wu, wd)


def _odd_mixer_kernel(x_ref, mod_ref, gain_ref, win_ref, ccw_ref, qn_ref, kn_ref, seg_ref,
                      cos_ref, sa_ref, sb_ref, yc_ref, q_ref, k_ref, v_ref, km_ref, cbuf):
    rows = x_ref.shape[1]

    @pl.when(pl.program_id(1) == 0)
    def _():
        cbuf[0:CONV_S_HALO, :] = jnp.zeros((CONV_S_HALO, MIX_W), F32)

    x = x_ref[0]
    h = _rms_mod(x, gain_ref[...], mod_ref[0, 1:2, :], mod_ref[0, 0:1, :])
    u = _dot(h.astype(BF16), win_ref[...])
    c_h = u[:, 0 * MIX_W:1 * MIX_W]
    c_b = u[:, 1 * MIX_W:2 * MIX_W]
    c_c = u[:, 2 * MIX_W:3 * MIX_W]
    q = u[:, 3 * MIX_W:4 * MIX_W]
    k = u[:, 4 * MIX_W:5 * MIX_W]
    v = u[:, 5 * MIX_W:6 * MIX_W]

    cbuf[CONV_S_HALO:CONV_S_HALO + rows, :] = c_c * c_h
    conv = _causal_conv(cbuf, CONV_S_HALO, rows, ccw_ref, SHORT_CONV_WIDTH)
    cbuf[0:CONV_S_HALO, :] = cbuf[rows:rows + CONV_S_HALO, :]
    yc_ref[0] = (c_b * conv).astype(yc_ref.dtype)

    seg = seg_ref[...]
    cos = cos_ref[...]
    sin_a = sa_ref[...]
    sin_b = sb_ref[...]

    def head_norm_rope(t, gain):
        t2 = t * t
        hi = t2.astype(BF16)
        lo = (t2 - hi.astype(F32)).astype(BF16)
        ssq = _dot(hi, seg) + _dot(lo, seg)
        t = t * lax.rsqrt(ssq * (1.0 / GROUP_DIM) + EPS) * gain
        outs = []
        for g in range(MIX_W // LANES):
            tg = t[:, g * LANES:(g + 1) * LANES]
            outs.append(tg * cos + pltpu.roll(tg, LANES - GROUP_DIM // 2, 1) * sin_a
                        + pltpu.roll(tg, GROUP_DIM // 2, 1) * sin_b)
        return jnp.concatenate(outs, axis=1)

    qr = head_norm_rope(q, qn_ref[...])
    kr = head_norm_rope(k, kn_ref[...])
    q_ref[0] = (qr * (GROUP_DIM ** -0.5)).astype(q_ref.dtype)
    k_ref[0] = kr.astype(k_ref.dtype)
    v_ref[0] = v.astype(v_ref.dtype)
    for i in range(rows // MOBA_BLOCK):
        km_ref[0, i] = jnp.mean(kr[i * MOBA_BLOCK:(i + 1) * MOBA_BLOCK, :], axis=0, keepdims=True)


def _odd_mixer(x, mod, gain, w_in, ccw, qn, kn, seg, cos, sin_a, sin_b):
    b, s, d = x.shape
    rows = min(MIX_ROWS, s)
    nb = s // MOBA_BLOCK
    row_spec = pl.BlockSpec((1, rows, d), lambda i, j: (i, j, 0))
    mix_spec = pl.BlockSpec((1, rows, MIX_W), lambda i, j: (i, j, 0))
    tab_spec = pl.BlockSpec((rows, LANES), lambda i, j: (j, 0))
    small = [gain, w_in, ccw, qn, kn, seg]
    mix_shape = jax.ShapeDtypeStruct((b, s, MIX_W), BF16)
    return pl.pallas_call(
        _odd_mixer_kernel,
        grid=(b, s // rows),
        in_specs=[row_spec, pl.BlockSpec((1, 6, d), lambda i, j: (i, 0, 0))]
                 + [_full(t.shape) for t in small] + [tab_spec, tab_spec, tab_spec],
        out_specs=[mix_spec, mix_spec, mix_spec, mix_spec,
                   pl.BlockSpec((1, rows // MOBA_BLOCK, 1, MIX_W), lambda i, j: (i, j, 0, 0))],
        out_shape=[mix_shape, mix_shape, mix_shape, mix_shape,
                   jax.ShapeDtypeStruct((b, nb, 1, MIX_W), F32)],
        scratch_shapes=[pltpu.VMEM((CONV_S_HALO + rows, MIX_W), F32)],
        compiler_params=_params("arbitrary", "arbitrary"),
        name="odd_mixer",
    )(x, mod, *small, cos, sin_a, sin_b)


def _top_blocks(gate, own):
    nblk = gate.shape[0]
    blk = lax.broadcasted_iota(jnp.int32, gate.shape, 0)
    remaining = blk < own
    sel = jnp.zeros(gate.shape, F32)
    for _ in range(MOBA_TOPK):
        gm = jnp.where(remaining, gate, -jnp.inf)
        top = jnp.max(gm, axis=0, keepdims=True)
        cand = remaining & (gm == top)
        first = jnp.min(jnp.where(cand, blk, nblk), axis=0, keepdims=True)
        pick = cand & (blk == first)
        sel = jnp.where(pick, 1.0, sel)
        remaining = remaining & jnp.logical_not(pick)
    return sel


def _moba_kernel(q_ref, k_ref, v_ref, km_ref, o_ref):
    own = pl.program_id(2)
    nblk = km_ref.shape[1]
    blk_rows = MOBA_BLOCK
    q = q_ref[0]
    lane = lax.broadcasted_iota(jnp.int32, q.shape, 1)
    zero = jnp.zeros_like(q)
    kmean = km_ref[0, :, 0, :].astype(BF16)
    row = lax.broadcasted_iota(jnp.int32, (blk_rows, blk_rows), 0)
    col = lax.broadcasted_iota(jnp.int32, (blk_rows, blk_rows), 1)
    causal_bias = jnp.where(col <= row, 0.0, -jnp.inf)
    pad = jnp.zeros((LANES - nblk, blk_rows), F32)

    q_heads, bias_heads = [], []
    for hh in range(LANES // GROUP_DIM):
        in_head = (lane >= hh * GROUP_DIM) & (lane < (hh + 1) * GROUP_DIM)
        qh = jnp.where(in_head, q, zero)
        sel_t = _top_blocks(_dot_nt(kmean, qh), own)
        sel = jnp.transpose(jnp.concatenate([sel_t, pad], axis=0))
        q_heads.append(qh)
        bias_heads.append(jnp.where(sel > 0.0, 0.0, -jnp.inf))

    group = MOBA_KEY_GROUP
    for g in range(-(-nblk // group)):
        nk = min((g + 1) * group, nblk)

        @pl.when(own // group == g)
        def _(g=g, nk=nk):
            keys = k_ref[0, 0:nk * blk_rows, :]
            vals = v_ref[0, 0:nk * blk_rows, :]
            outs = []
            for qh, bias in zip(q_heads, bias_heads):
                s = _dot_nt(qh, keys)
                pieces = []
                for jb in range(nk):
                    blk_bias = bias[:, jb:jb + 1]
                    if jb >= g * group:
                        blk_bias = jnp.where(own == jb, causal_bias, blk_bias)
                    pieces.append(s[:, jb * blk_rows:(jb + 1) * blk_rows] + blk_bias)
                s = jnp.concatenate(pieces, axis=1)
                m = jnp.max(s, axis=1, keepdims=True)
                p = jnp.exp(s - m)
                l = jnp.sum(p, axis=1, keepdims=True)
                outs.append(_dot(p.astype(BF16), vals) / l)
            o_ref[0] = jnp.where(lane < GROUP_DIM, outs[0], outs[1]).astype(o_ref.dtype)


def _moba(q, k, v, kmean):
    b, s, _ = q.shape
    nb = s // MOBA_BLOCK
    q_spec = pl.BlockSpec((1, MOBA_BLOCK, LANES), lambda i, p, j: (i, j, p))
    kv_spec = pl.BlockSpec((1, s, LANES), lambda i, p, j: (i, 0, p))
    return pl.pallas_call(
        _moba_kernel,
        grid=(b, MIX_W // LANES, nb),
        in_specs=[q_spec, kv_spec, kv_spec,
                  pl.BlockSpec((1, nb, 1, LANES), lambda i, p, j: (i, 0, 0, p))],
        out_specs=q_spec,
        out_shape=jax.ShapeDtypeStruct(q.shape, BF16),
        compiler_params=_params("arbitrary", "arbitrary", "arbitrary"),
        name="moba_attention",
    )(q, k, v, kmean)


def _odd_out_kernel(yc_ref, yd_ref, x_ref, mod_ref, wout_ref, gain_ref, rwh_ref, rwl_ref, rb_ref,
                    x2_ref, h2_ref, wt_ref, mk_ref, cnt_ref):
    x = x_ref[0]
    mix = _dot(yc_ref[0], wout_ref[0:MIX_W, :]) + _dot(yd_ref[0], wout_ref[MIX_W:2 * MIX_W, :])
    x2 = x + mod_ref[0, 2:3, :] * mix
    x2_ref[0] = x2
    h = _rms_mod(x2, gain_ref[...], mod_ref[0, 4:5, :], mod_ref[0, 3:4, :])
    h_hi = h.astype(BF16)
    h2_ref[0] = h_hi.astype(h2_ref.dtype)

    h_lo = (h - h_hi.astype(F32)).astype(BF16)
    rwh = rwh_ref[...]
    logits = _dot(h_hi, rwh) + _dot(h_lo, rwh) + _dot(h_hi, rwl_ref[...]) + rb_ref[...]
    lane = lax.broadcasted_iota(jnp.int32, logits.shape, 1)
    valid = lane < N_EXPERTS
    lg = jnp.where(valid, logits, -jnp.inf)
    top1 = jnp.max(lg, axis=1, keepdims=True)
    idx1 = jnp.min(jnp.where(lg == top1, lane, LANES), axis=1, keepdims=True)
    pick1 = lane == idx1
    lg2 = jnp.where(pick1, -jnp.inf, lg)
    top2 = jnp.max(lg2, axis=1, keepdims=True)
    idx2 = jnp.min(jnp.where((lg2 == top2) & valid & jnp.logical_not(pick1), lane, LANES), axis=1, keepdims=True)
    pick2 = lane == idx2
    e2 = jnp.exp(top2 - top1)
    den = 1.0 + e2
    wt_ref[0] = jnp.where(pick1, 1.0 / den, 0.0) + jnp.where(pick2, e2 / den, 0.0)
    mask = jnp.where(pick1 | pick2, 1.0, 0.0)
    mk_ref[0] = mask.astype(mk_ref.dtype)
    cnt_ref[0, 0] = jnp.sum(mask, axis=0, keepdims=True)


def _odd_out(yc, yd, x, mod, w_out, gain, rw_hi, rw_lo, rb):
    b, s, d = x.shape
    rows = min(MIX_ROWS, s)
    row_spec = pl.BlockSpec((1, rows, d), lambda i, j: (i, j, 0))
    mix_spec = pl.BlockSpec((1, rows, MIX_W), lambda i, j: (i, j, 0))
    lane_spec = pl.BlockSpec((1, rows, LANES), lambda i, j: (i, j, 0))
    small = [w_out, gain, rw_hi, rw_lo, rb]
    return pl.pallas_call(
        _odd_out_kernel,
        grid=(b, s // rows),
        in_specs=[mix_spec, mix_spec, row_spec, pl.BlockSpec((1, 6, d), lambda i, j: (i, 0, 0))]
                 + [_full(t.shape) for t in small],
        out_specs=[row_spec, row_spec, lane_spec, lane_spec,
                   pl.BlockSpec((1, 1, 1, LANES), lambda i, j: (i, j, 0, 0))],
        out_shape=[jax.ShapeDtypeStruct(x.shape, F32),
                   jax.ShapeDtypeStruct(x.shape, BF16),
                   jax.ShapeDtypeStruct((b, s, LANES), F32),
                   jax.ShapeDtypeStruct((b, s, LANES), BF16),
                   jax.ShapeDtypeStruct((b, s // rows, 1, LANES), F32)],
        compiler_params=_params("arbitrary", "arbitrary"),
        name="odd_out_router",
    )(yc, yd, x, mod, *small)


def _moe_kernel(cnt_ref, h2_ref, wt_ref, mk_ref, x2_ref, mod_ref, wg_ref, wu_ref, wd_ref, o_ref,
                posc_scr, posr_scr, xs_scr, acc_scr, posb_scr, wb_scr):
    tile = pl.program_id(0)
    e = pl.program_id(1)
    f = pl.program_id(2)
    rows = h2_ref.shape[0]
    ch = MOE_CHUNK
    big, small = MOE_BIG_ROWS, MOE_SMALL_ROWS
    count = cnt_ref[tile * N_EXPERTS + e]
    n_big = lax.shift_right_logical(count + (small - 1), big.bit_length() - 1)
    small_base = n_big * big
    n_small = lax.shift_right_logical(jnp.maximum(count - small_base, 0) + (small - 1), small.bit_length() - 1)

    def for_blocks(body):
        lax.fori_loop(0, n_big, lambda kk, c: body(pl.multiple_of(kk * big, big), big) or c, 0)
        lax.fori_loop(0, n_small,
                      lambda kk, c: body(pl.multiple_of(small_base + kk * small, small), small) or c, 0)

    @pl.when((e == 0) & (f == 0))
    def _positions():
        o_ref[...] = jnp.zeros(o_ref.shape, F32)
        r_i = lax.broadcasted_iota(jnp.int32, (ch, ch), 0)
        c_i = lax.broadcasted_iota(jnp.int32, (ch, ch), 1)
        lower = jnp.where(c_i < r_i, 1.0, 0.0).astype(BF16)
        upper = jnp.where(r_i < c_i, 1.0, 0.0).astype(BF16)
        eye = jnp.where(lax.broadcasted_iota(jnp.int32, (N_EXPERTS, LANES), 0)
                        == lax.broadcasted_iota(jnp.int32, (N_EXPERTS, LANES), 1), 1.0, 0.0).astype(BF16)
        carry_c = jnp.zeros((1, LANES), F32)
        carry_r = jnp.zeros((N_EXPERTS, 1), F32)
        for blk in range(rows // ch):
            sl = slice(blk * ch, (blk + 1) * ch)
            mb = mk_ref[sl, :]
            mbf = mb.astype(F32)
            posc_scr[sl, :] = jnp.where(mbf > 0.0, _dot(lower, mb) + carry_c, -1.0)
            carry_c = carry_c + jnp.sum(mbf, axis=0, keepdims=True)
            mbt = _dot_nt(eye, mb)
            posr_scr[:, sl] = jnp.where(mbt > 0.0, _dot(mbt.astype(BF16), upper) + carry_r, -1.0)
            carry_r = carry_r + jnp.sum(mbt, axis=1, keepdims=True)

    @pl.when(f == 0)
    def _gather():
        posr = posr_scr[pl.ds(e, 1), :]

        def body(off, m):
            r_i = lax.broadcasted_iota(jnp.int32, (m, rows), 0).astype(F32)
            onehot = jnp.where(posr - off.astype(F32) == r_i, 1.0, 0.0).astype(BF16)
            xs_scr[pl.ds(off, m), :] = _dot(onehot, h2_ref[...]).astype(xs_scr.dtype)
            acc_scr[pl.ds(off, m), :] = jnp.zeros((m, acc_scr.shape[1]), F32)

        for_blocks(body)

    def ffn_body(off, m):
        xk = xs_scr[pl.ds(off, m), :]
        g = _dot(xk, wg_ref[0])
        act = (g * _sigmoid(g)) * _dot(xk, wu_ref[0])
        acc_scr[pl.ds(off, m), :] += _dot(act.astype(BF16), wd_ref[0])

    for_blocks(ffn_body)

    @pl.when(f == pl.num_programs(2) - 1)
    def _scatter():
        sub = MOE_SCATTER_ROWS
        lane = lax.broadcasted_iota(jnp.int32, (sub, LANES), 1)
        lane_f = lane.astype(F32)
        for tb in range(rows // sub):
            sl = slice(tb * sub, (tb + 1) * sub)
            pos_e = jnp.sum(jnp.where(lane == e, posc_scr[sl, :], 0.0), axis=1, keepdims=True)
            w_e = jnp.sum(jnp.where(lane == e, wt_ref[sl, :], 0.0), axis=1, keepdims=True)
            posb_scr[sl, :] = jnp.broadcast_to(pos_e, (sub, LANES)) - lane_f
            wb_scr[sl, :] = jnp.broadcast_to(w_e, (sub, LANES))

        def body(off, m):
            y = acc_scr[pl.ds(off, m), :].astype(BF16)
            off_f = off.astype(F32)
            reps = acc_scr.shape[1] // LANES
            for tb in range(rows // sub):
                sl = slice(tb * sub, (tb + 1) * sub)
                rel = posb_scr[sl, :] - off_f
                onehot = jnp.concatenate(
                    [jnp.where(rel == float(g * LANES), 1.0, 0.0) for g in range(m // LANES)], axis=1).astype(BF16)
                o_ref[sl, :] += jnp.concatenate([wb_scr[sl, :]] * reps, axis=1) * _dot(onehot, y)

        for_blocks(body)

        @pl.when(e == N_EXPERTS - 1)
        def _residual():
            o_ref[...] = x2_ref[...] + mod_ref[0, 5:6, :] * o_ref[...]


def _moe(h2, wts, mask, x2, mod, counts, wg, wu, wd, seq):
    t, d = h2.shape
    fdim = wg.shape[2]
    rows = min(MOE_ROWS, seq)
    cols = MOE_COLS if fdim % MOE_COLS == 0 else fdim
    once = pl.Buffered(1)
    tile_spec = lambda width: pl.BlockSpec((rows, width), lambda i, e, f, n: (i, 0), pipeline_mode=once)
    grid_spec = pltpu.PrefetchScalarGridSpec(
        num_scalar_prefetch=1,
        grid=(t // rows, N_EXPERTS, fdim // cols),
        in_specs=[tile_spec(d), tile_spec(LANES), tile_spec(LANES), tile_spec(d),
                  pl.BlockSpec((1, 6, d), lambda i, e, f, n: (i * rows // seq, 0, 0)),
                  pl.BlockSpec((1, d, cols), lambda i, e, f, n: (e, 0, f)),
                  pl.BlockSpec((1, d, cols), lambda i, e, f, n: (e, 0, f)),
                  pl.BlockSpec((1, cols, d), lambda i, e, f, n: (e, f, 0))],
        out_specs=pl.BlockSpec((rows, d), lambda i, e, f, n: (i, 0)),
        scratch_shapes=[pltpu.VMEM((rows, LANES), F32),
                        pltpu.VMEM((N_EXPERTS, rows), F32),
                        pltpu.VMEM((rows, d), BF16),
                        pltpu.VMEM((rows, d), F32),
                        pltpu.VMEM((rows, LANES), F32),
                        pltpu.VMEM((rows, LANES), F32)],
    )
    return pl.pallas_call(
        _moe_kernel,
        grid_spec=grid_spec,
        out_shape=jax.ShapeDtypeStruct((t, d), F32),
        compiler_params=_params("arbitrary", "arbitrary", "arbitrary"),
        name="moe_experts",
    )(counts, h2, wts, mask, x2, mod, wg, wu, wd)


def _block_diag(w):
    h, i, j = w.shape
    eye = jnp.eye(h, dtype=w.dtype)
    return jnp.einsum('hij,hk->hikj', w, eye).reshape(h * i, h * j)


def _rope_tables(seq):
    half = GROUP_DIM // 2
    inv = ROPE_THETA ** (-jnp.arange(half, dtype=F32) / half)
    ang = jnp.arange(seq).astype(F32)[:, None] * inv[None, :]
    cos, sin, zero = jnp.cos(ang), jnp.sin(ang), jnp.zeros_like(ang)
    reps = LANES // GROUP_DIM
    return (jnp.tile(jnp.concatenate([cos, cos], axis=1), (1, reps)),
            jnp.tile(jnp.concatenate([-sin, zero], axis=1), (1, reps)),
            jnp.tile(jnp.concatenate([zero, sin], axis=1), (1, reps)))


def _row(v):
    return v.reshape(1, -1).astype(F32)


def kernel(x, c, e_ada_w, e_ada_b, e_norm_mix, e_norm_ffn, e_w_in, e_conv_a_w, e_conv_a_b,
           e_ln_a_g, e_ln_a_b, e_conv_b_w, e_conv_b_b, e_lru_wa, e_lru_ba, e_lru_wx, e_lru_bx,
           e_lru_lambda, e_w_out, e_ffn_wg, e_ffn_wu, e_ffn_wd,
           o_ada_w, o_ada_b, o_norm_mix, o_norm_ffn, o_w_in, o_conv_c_w, o_q_norm, o_k_norm,
           o_w_out, o_router_w, o_router_b, o_moe_wg, o_moe_wu, o_moe_wd):
    b, s, d = x.shape
    assert s % MOBA_BLOCK == 0 and d % LANES == 0
    c_pad = jnp.zeros((SUBLANES, d), F32).at[:b].set(c.astype(F32))
    seg = jnp.kron(jnp.eye(N_GROUPS, dtype=F32), jnp.ones((GROUP_DIM, GROUP_DIM), F32)).astype(BF16)
    cos, sin_a, sin_b = _rope_tables(s)

    def modulation(w, bias):
        return _modulation(c_pad, w, _row(bias))[:b].reshape(b, 6, d)

    x = x.astype(F32)
    for layer in range(DEPTH):
        j = layer // 2
        if layer % 2 == 0:
            mod = modulation(e_ada_w[j], e_ada_b[j])
            x = _even_mixer(
                x, mod, _row(e_norm_mix[j]), e_w_in[j].astype(BF16),
                e_conv_a_w[j].astype(F32), _row(e_conv_a_b[j]), _row(e_ln_a_g[j]), _row(e_ln_a_b[j]),
                e_conv_b_w[j].astype(F32), _row(e_conv_b_b[j]),
                _block_diag(e_lru_wa[j]).astype(BF16), _row(e_lru_ba[j]),
                _block_diag(e_lru_wx[j]).astype(BF16), _row(e_lru_bx[j]),
                _row(e_lru_lambda[j]), e_w_out[j].astype(BF16))
            x = _dense_ffn(x, mod, _row(e_norm_ffn[j]), e_ffn_wg[j].astype(BF16),
                           e_ffn_wu[j].astype(BF16), e_ffn_wd[j].astype(BF16))
        else:
            mod = modulation(o_ada_w[j], o_ada_b[j])
            yc, q, k, v, kmean = _odd_mixer(
                x, mod, _row(o_norm_mix[j]), o_w_in[j].astype(BF16), o_conv_c_w[j].astype(F32),
                _row(jnp.tile(o_q_norm[j], N_GROUPS)), _row(jnp.tile(o_k_norm[j], N_GROUPS)),
                seg, cos, sin_a, sin_b)
            yd = _moba(q, k, v, kmean)
            rw = jnp.zeros((d, LANES), F32).at[:, :N_EXPERTS].set(o_router_w[j].astype(F32))
            rw_hi = rw.astype(BF16)
            rw_lo = (rw - rw_hi.astype(F32)).astype(BF16)
            rb = jnp.zeros((1, LANES), F32).at[0, :N_EXPERTS].set(o_router_b[j].astype(F32))
            x2, h2, wts, mask, cnt = _odd_out(yc, yd, x, mod, o_w_out[j].astype(BF16),
                                              _row(o_norm_ffn[j]), rw_hi, rw_lo, rb)
            rows = min(MOE_ROWS, s)
            per_tile = cnt.reshape(b * s // rows, -1, LANES)[:, :, :N_EXPERTS].sum(axis=1)
            counts = per_tile.astype(jnp.int32).reshape(-1)
            out = _moe(h2.reshape(b * s, d), wts.reshape(b * s, LANES), mask.reshape(b * s, LANES),
                       x2.reshape(b * s, d), mod, counts,
                       o_moe_wg[j].astype(BF16), o_moe_wu[j].astype(BF16), o_moe_wd[j].astype(BF16), s)
            x = out.reshape(b, s, d)
    return x
```

```python
import functools

import jax
import jax.numpy as jnp
from jax import lax
from jax.experimental import pallas as pl
from jax.experimental.pallas import tpu as pltpu

F32 = jnp.float32
BF16 = jnp.bfloat16

N_GROUPS = 8
GROUP_DIM = 64
MIX_W = N_GROUPS * GROUP_DIM
CONF_WIDTH = 31
LRU_CONV_WIDTH = 4
LRU_C = 8.0
SHORT_CONV_WIDTH = 3
MOBA_BLOCK = 256
MOBA_TOPK = 3
ROPE_THETA = 10000.0
N_EXPERTS = 8
EPS = 1e-6
DEPTH = 2

LANES = 128
SUBLANES = 8
VMEM_LIMIT_BYTES = 56 * 1024 * 1024

MIX_ROWS = 512
FFN_ROWS = 512
FFN_COLS = 2816
MOE_ROWS = 2048
MOE_COLS = 512
MOE_CHUNK = 256
MOE_BIG_ROWS = 512
MOE_MED_ROWS = 256
MOE_SMALL_ROWS = 128
MOE_SCATTER_ROWS = 512
MOBA_KEY_GROUP = 2
CONV_A_HALO = 32
CONV_S_HALO = 8


def _dot(a, b):
    return jnp.dot(a, b, preferred_element_type=F32)


def _dot_nt(a, b):
    return lax.dot_general(a, b, (((1,), (1,)), ((), ())), preferred_element_type=F32)


def _sigmoid(x):
    return 1.0 / (1.0 + jnp.exp(-x))


def _rms_mod(x, gain, scale, shift):
    ms = jnp.mean(x * x, axis=-1, keepdims=True)
    return (x * lax.rsqrt(ms + EPS)) * gain * (1.0 + scale) + shift


def _params(*sem):
    return pltpu.CompilerParams(dimension_semantics=sem, vmem_limit_bytes=VMEM_LIMIT_BYTES)


def _full(shape):
    n = len(shape)
    return pl.BlockSpec(shape, lambda *_: (0,) * n)


def _mod_kernel(c_ref, w_ref, b_ref, o_ref):
    c = c_ref[...]
    c_act = c * _sigmoid(c)
    o_ref[...] = _dot(c_act.astype(BF16), w_ref[...].astype(BF16)) + b_ref[...]


def _modulation(c_pad, w, b):
    d, n = w.shape
    tn = n // 4
    return pl.pallas_call(
        _mod_kernel,
        grid=(4,),
        in_specs=[_full(c_pad.shape),
                  pl.BlockSpec((d, tn), lambda j: (0, j)),
                  pl.BlockSpec((1, tn), lambda j: (0, j))],
        out_specs=pl.BlockSpec((c_pad.shape[0], tn), lambda j: (0, j)),
        out_shape=jax.ShapeDtypeStruct((c_pad.shape[0], n), F32),
        compiler_params=_params("arbitrary"),
        name="adaln_mod",
    )(c_pad, w, b)


def _causal_conv(buf, halo, rows, w_ref, width):
    total = halo + rows
    full = buf[0:total, :]
    acc = None
    for phase in range(SUBLANES):
        taps = [j for j in range(width) if (halo - (width - 1) + j) % SUBLANES == phase]
        if not taps:
            continue
        shifted = full if phase == 0 else pltpu.roll(full, total - phase, 0)
        for j in taps:
            start = halo - (width - 1) + j - phase
            term = w_ref[j:j + 1, :] * shifted[start:start + rows, :]
            acc = term if acc is None else acc + term
    return acc


def _even_mixer_kernel(x_ref, mod_ref, gain_ref, win_ref, caw_ref, cab_ref, lng_ref, lnb_ref,
                       cbw_ref, cbb_ref, wa_ref, ba_ref, wx_ref, bx_ref, lam_ref, wout_ref,
                       o_ref, abuf, bbuf, hcar):
    rows = x_ref.shape[1]

    @pl.when(pl.program_id(1) == 0)
    def _():
        abuf[0:CONV_A_HALO, :] = jnp.zeros((CONV_A_HALO, MIX_W), F32)
        bbuf[0:CONV_S_HALO, :] = jnp.zeros((CONV_S_HALO, MIX_W), F32)
        hcar[...] = jnp.zeros(hcar.shape, F32)

    x = x_ref[0]
    h = _rms_mod(x, gain_ref[...], mod_ref[0, 1:2, :], mod_ref[0, 0:1, :])
    u = _dot(h.astype(BF16), win_ref[...])
    a_val = u[:, 0 * MIX_W:1 * MIX_W]
    a_gate = u[:, 1 * MIX_W:2 * MIX_W]
    b_x = u[:, 2 * MIX_W:3 * MIX_W]
    b_gate = u[:, 3 * MIX_W:4 * MIX_W]

    abuf[CONV_A_HALO:CONV_A_HALO + rows, :] = a_val * _sigmoid(a_gate)
    ya = _causal_conv(abuf, CONV_A_HALO, rows, caw_ref, CONF_WIDTH) + cab_ref[...]
    abuf[0:CONV_A_HALO, :] = abuf[rows:rows + CONV_A_HALO, :]
    mu = jnp.mean(ya, axis=-1, keepdims=True)
    dev = ya - mu
    var = jnp.mean(dev * dev, axis=-1, keepdims=True)
    ya = dev * lax.rsqrt(var + EPS) * lng_ref[...] + lnb_ref[...]
    ya = ya * _sigmoid(ya)

    bbuf[CONV_S_HALO:CONV_S_HALO + rows, :] = b_x
    xb = _causal_conv(bbuf, CONV_S_HALO, rows, cbw_ref, LRU_CONV_WIDTH) + cbb_ref[...]
    bbuf[0:CONV_S_HALO, :] = bbuf[rows:rows + CONV_S_HALO, :]
    xb16 = xb.astype(BF16)
    r_gate = _sigmoid(_dot(xb16, wa_ref[...]) + ba_ref[...])
    i_gate = _sigmoid(_dot(xb16, wx_ref[...]) + bx_ref[...])
    lam = lam_ref[...]
    log_sig = -(jnp.maximum(-lam, 0.0) + jnp.log1p(jnp.exp(-jnp.abs(lam))))
    log_a = LRU_C * r_gate * log_sig
    a = jnp.exp(log_a)
    mult = jnp.sqrt(-jnp.tanh(log_a) * (1.0 + a * a))
    bterm = mult * (i_gate * xb)

    row = lax.broadcasted_iota(jnp.int32, (rows, MIX_W), 0)
    d = 1
    while d < rows:
        keep = row >= d
        a_prev = jnp.where(keep, pltpu.roll(a, d, 0), 1.0)
        b_prev = jnp.where(keep, pltpu.roll(bterm, d, 0), 0.0)
        bterm = bterm + a * b_prev
        a = a * a_prev
        d *= 2
    hseq = bterm + a * hcar[0:1, :]
    hcar[...] = jnp.broadcast_to(hseq[rows - 1:rows, :], hcar.shape)

    gelu = 0.5 * b_gate * (1.0 + jnp.tanh(0.7978845608028654 * (b_gate + 0.044715 * (b_gate * b_gate * b_gate))))
    yb = hseq * gelu

    mix = _dot(ya.astype(BF16), wout_ref[0:MIX_W, :]) + _dot(yb.astype(BF16), wout_ref[MIX_W:2 * MIX_W, :])
    o_ref[0] = x + mod_ref[0, 2:3, :] * mix


def _even_mixer(x, mod, gain, w_in, caw, cab, lng, lnb, cbw, cbb, wa, ba, wx, bx, lam, w_out):
    b, s, d = x.shape
    rows = min(MIX_ROWS, s)
    row_spec = pl.BlockSpec((1, rows, d), lambda i, j: (i, j, 0))
    small = [gain, w_in, caw, cab, lng, lnb, cbw, cbb, wa, ba, wx, bx, lam, w_out]
    return pl.pallas_call(
        _even_mixer_kernel,
        grid=(b, s // rows),
        in_specs=[row_spec, pl.BlockSpec((1, 6, d), lambda i, j: (i, 0, 0))] + [_full(t.shape) for t in small],
        out_specs=row_spec,
        out_shape=jax.ShapeDtypeStruct(x.shape, F32),
        scratch_shapes=[pltpu.VMEM((CONV_A_HALO + rows, MIX_W), F32),
                        pltpu.VMEM((CONV_S_HALO + rows, MIX_W), F32),
                        pltpu.VMEM((SUBLANES, MIX_W), F32)],
        compiler_params=_params("arbitrary", "arbitrary"),
        name="even_mixer",
    )(x, mod, *small)


def _ffn_kernel(x_ref, mod_ref, gain_ref, wg_ref, wu_ref, wd_ref, o_ref, h_scr):
    @pl.when(pl.program_id(2) == 0)
    def _():
        x = x_ref[0]
        h_scr[...] = _rms_mod(x, gain_ref[...], mod_ref[0, 4:5, :], mod_ref[0, 3:4, :]).astype(BF16)
        o_ref[0] = x

    h = h_scr[...]
    g = _dot(h, wg_ref[...])
    act = (g * _sigmoid(g)) * _dot(h, wu_ref[...])
    o_ref[0] += mod_ref[0, 5:6, :] * _dot(act.astype(BF16), wd_ref[...])


def _dense_ffn(x, mod, gain, wg, wu, wd):
    b, s, d = x.shape
    f = wg.shape[1]
    rows = min(FFN_ROWS, s)
    cols = FFN_COLS if f % FFN_COLS == 0 else f
    row_spec = pl.BlockSpec((1, rows, d), lambda i, j, k: (i, j, 0))
    return pl.pallas_call(
        _ffn_kernel,
        grid=(b, s // rows, f // cols),
        in_specs=[row_spec,
                  pl.BlockSpec((1, 6, d), lambda i, j, k: (i, 0, 0)),
                  _full(gain.shape),
                  pl.BlockSpec((d, cols), lambda i, j, k: (0, k)),
                  pl.BlockSpec((d, cols), lambda i, j, k: (0, k)),
                  pl.BlockSpec((cols, d), lambda i, j, k: (k, 0))],
        out_specs=row_spec,
        out_shape=jax.ShapeDtypeStruct(x.shape, F32),
        scratch_shapes=[pltpu.VMEM((rows, d), BF16)],
        compiler_params=_params("arbitrary", "arbitrary", "arbitrary"),
        name="dense_ffn",
    )(x, mod, gain, wg, wu, wd)


def _odd_mixer_kernel(x_ref, mod_ref, gain_ref, win_ref, ccw_ref, qn_ref, kn_ref, seg_ref,
                      cos_ref, sa_ref, sb_ref, yc_ref, q_ref, k_ref, v_ref, km_ref, cbuf):
    rows = x_ref.shape[1]

    @pl.when(pl.program_id(1) == 0)
    def _():
        cbuf[0:CONV_S_HALO, :] = jnp.zeros((CONV_S_HALO, MIX_W), F32)

    x = x_ref[0]
    h = _rms_mod(x, gain_ref[...], mod_ref[0, 1:2, :], mod_ref[0, 0:1, :])
    u = _dot(h.astype(BF16), win_ref[...])
    c_h = u[:, 0 * MIX_W:1 * MIX_W]
    c_b = u[:, 1 * MIX_W:2 * MIX_W]
    c_c = u[:, 2 * MIX_W:3 * MIX_W]
    q = u[:, 3 * MIX_W:4 * MIX_W]
    k = u[:, 4 * MIX_W:5 * MIX_W]
    v = u[:, 5 * MIX_W:6 * MIX_W]

    cbuf[CONV_S_HALO:CONV_S_HALO + rows, :] = c_c * c_h
    conv = _causal_conv(cbuf, CONV_S_HALO, rows, ccw_ref, SHORT_CONV_WIDTH)
    cbuf[0:CONV_S_HALO, :] = cbuf[rows:rows + CONV_S_HALO, :]
    yc_ref[0] = (c_b * conv).astype(yc_ref.dtype)

    seg = seg_ref[...]
    cos = cos_ref[...]
    sin_a = sa_ref[...]
    sin_b = sb_ref[...]

    def head_norm_rope(t, gain):
        t2 = t * t
        hi = t2.astype(BF16)
        lo = (t2 - hi.astype(F32)).astype(BF16)
        ssq = _dot(hi, seg) + _dot(lo, seg)
        t = t * lax.rsqrt(ssq * (1.0 / GROUP_DIM) + EPS) * gain
        outs = []
        for g in range(MIX_W // LANES):
            tg = t[:, g * LANES:(g + 1) * LANES]
            outs.append(tg * cos + pltpu.roll(tg, LANES - GROUP_DIM // 2, 1) * sin_a
                        + pltpu.roll(tg, GROUP_DIM // 2, 1) * sin_b)
        return jnp.concatenate(outs, axis=1)

    qr = head_norm_rope(q, qn_ref[...])
    kr = head_norm_rope(k, kn_ref[...])
    q_ref[0] = (qr * (GROUP_DIM ** -0.5)).astype(q_ref.dtype)
    k_ref[0] = kr.astype(k_ref.dtype)
    v_ref[0] = v.astype(v_ref.dtype)
    for i in range(rows // MOBA_BLOCK):
        km_ref[0, i] = jnp.mean(kr[i * MOBA_BLOCK:(i + 1) * MOBA_BLOCK, :], axis=0, keepdims=True)


def _odd_mixer(x, mod, gain, w_in, ccw, qn, kn, seg, cos, sin_a, sin_b):
    b, s, d = x.shape
    rows = min(MIX_ROWS, s)
    nb = s // MOBA_BLOCK
    row_spec = pl.BlockSpec((1, rows, d), lambda i, j: (i, j, 0))
    mix_spec = pl.BlockSpec((1, rows, MIX_W), lambda i, j: (i, j, 0))
    tab_spec = pl.BlockSpec((rows, LANES), lambda i, j: (j, 0))
    small = [gain, w_in, ccw, qn, kn, seg]
    mix_shape = jax.ShapeDtypeStruct((b, s, MIX_W), BF16)
    return pl.pallas_call(
        _odd_mixer_kernel,
        grid=(b, s // rows),
        in_specs=[row_spec, pl.BlockSpec((1, 6, d), lambda i, j: (i, 0, 0))]
                 + [_full(t.shape) for t in small] + [tab_spec, tab_spec, tab_spec],
        out_specs=[mix_spec, mix_spec, mix_spec, mix_spec,
                   pl.BlockSpec((1, rows // MOBA_BLOCK, 1, MIX_W), lambda i, j: (i, j, 0, 0))],
        out_shape=[mix_shape, mix_shape, mix_shape, mix_shape,
                   jax.ShapeDtypeStruct((b, nb, 1, MIX_W), F32)],
        scratch_shapes=[pltpu.VMEM((CONV_S_HALO + rows, MIX_W), F32)],
        compiler_params=_params("arbitrary", "arbitrary"),
        name="odd_mixer",
    )(x, mod, *small, cos, sin_a, sin_b)


def _top_blocks(gate, own):
    nblk = gate.shape[0]
    blk = lax.broadcasted_iota(jnp.int32, gate.shape, 0)
    remaining = blk < own
    sel = jnp.zeros(gate.shape, F32)
    for _ in range(MOBA_TOPK):
        gm = jnp.where(remaining, gate, -jnp.inf)
        top = jnp.max(gm, axis=0, keepdims=True)
        cand = remaining & (gm == top)
        first = jnp.min(jnp.where(cand, blk, nblk), axis=0, keepdims=True)
        pick = cand & (blk == first)
        sel = jnp.where(pick, 1.0, sel)
        remaining = remaining & jnp.logical_not(pick)
    return sel


def _moba_kernel(q_ref, k_ref, v_ref, km_ref, o_ref):
    own = pl.program_id(2)
    nblk = km_ref.shape[1]
    blk_rows = MOBA_BLOCK
    q = q_ref[0]
    lane = lax.broadcasted_iota(jnp.int32, q.shape, 1)
    zero = jnp.zeros_like(q)
    kmean = km_ref[0, :, 0, :].astype(BF16)
    row = lax.broadcasted_iota(jnp.int32, (blk_rows, blk_rows), 0)
    col = lax.broadcasted_iota(jnp.int32, (blk_rows, blk_rows), 1)
    causal_bias = jnp.where(col <= row, 0.0, -jnp.inf)
    pad = jnp.zeros((LANES - nblk, blk_rows), F32)

    q_heads, bias_heads = [], []
    for hh in range(LANES // GROUP_DIM):
        in_head = (lane >= hh * GROUP_DIM) & (lane < (hh + 1) * GROUP_DIM)
        qh = jnp.where(in_head, q, zero)
        sel_t = _top_blocks(_dot_nt(kmean, qh), own)
        sel = jnp.transpose(jnp.concatenate([sel_t, pad], axis=0))
        q_heads.append(qh)
        bias_heads.append(jnp.where(sel > 0.0, 0.0, -jnp.inf))

    group = MOBA_KEY_GROUP
    for g in range(-(-nblk // group)):
        nk = min((g + 1) * group, nblk)

        @pl.when(own // group == g)
        def _(g=g, nk=nk):
            keys = k_ref[0, 0:nk * blk_rows, :]
            vals = v_ref[0, 0:nk * blk_rows, :]
            outs = []
            for qh, bias in zip(q_heads, bias_heads):
                s = _dot_nt(qh, keys)
                pieces = []
                for jb in range(nk):
                    blk_bias = bias[:, jb:jb + 1]
                    if jb >= g * group:
                        blk_bias = jnp.where(own == jb, causal_bias, blk_bias)
                    pieces.append(s[:, jb * blk_rows:(jb + 1) * blk_rows] + blk_bias)
                s = jnp.concatenate(pieces, axis=1)
                m = jnp.max(s, axis=1, keepdims=True)
                p = jnp.exp(s - m)
                l = jnp.sum(p, axis=1, keepdims=True)
                outs.append(_dot(p.astype(BF16), vals) / l)
            o_ref[0] = jnp.where(lane < GROUP_DIM, outs[0], outs[1]).astype(o_ref.dtype)


def _moba(q, k, v, kmean):
    b, s, _ = q.shape
    nb = s // MOBA_BLOCK
    q_spec = pl.BlockSpec((1, MOBA_BLOCK, LANES), lambda i, p, j: (i, j, p))
    kv_spec = pl.BlockSpec((1, s, LANES), lambda i, p, j: (i, 0, p))
    return pl.pallas_call(
        _moba_kernel,
        grid=(b, MIX_W // LANES, nb),
        in_specs=[q_spec, kv_spec, kv_spec,
                  pl.BlockSpec((1, nb, 1, LANES), lambda i, p, j: (i, 0, 0, p))],
        out_specs=q_spec,
        out_shape=jax.ShapeDtypeStruct(q.shape, BF16),
        compiler_params=_params("arbitrary", "arbitrary", "arbitrary"),
        name="moba_attention",
    )(q, k, v, kmean)


def _odd_out_kernel(yc_ref, yd_ref, x_ref, mod_ref, wout_ref, gain_ref, rwh_ref, rwl_ref, rb_ref,
                    x2_ref, h2_ref, wt_ref, mk_ref, cnt_ref):
    x = x_ref[0]
    mix = _dot(yc_ref[0], wout_ref[0:MIX_W, :]) + _dot(yd_ref[0], wout_ref[MIX_W:2 * MIX_W, :])
    x2 = x + mod_ref[0, 2:3, :] * mix
    x2_ref[0] = x2
    h = _rms_mod(x2, gain_ref[...], mod_ref[0, 4:5, :], mod_ref[0, 3:4, :])
    h_hi = h.astype(BF16)
    h2_ref[0] = h_hi.astype(h2_ref.dtype)

    h_lo = (h - h_hi.astype(F32)).astype(BF16)
    rwh = rwh_ref[...]
    logits = _dot(h_hi, rwh) + _dot(h_lo, rwh) + _dot(h_hi, rwl_ref[...]) + rb_ref[...]
    lane = lax.broadcasted_iota(jnp.int32, logits.shape, 1)
    valid = lane < N_EXPERTS
    lg = jnp.where(valid, logits, -jnp.inf)
    top1 = jnp.max(lg, axis=1, keepdims=True)
    idx1 = jnp.min(jnp.where(lg == top1, lane, LANES), axis=1, keepdims=True)
    pick1 = lane == idx1
    lg2 = jnp.where(pick1, -jnp.inf, lg)
    top2 = jnp.max(lg2, axis=1, keepdims=True)
    idx2 = jnp.min(jnp.where((lg2 == top2) & valid & jnp.logical_not(pick1), lane, LANES), axis=1, keepdims=True)
    pick2 = lane == idx2
    e2 = jnp.exp(top2 - top1)
    den = 1.0 + e2
    wt_ref[0] = jnp.where(pick1, 1.0 / den, 0.0) + jnp.where(pick2, e2 / den, 0.0)
    mask = jnp.where(pick1 | pick2, 1.0, 0.0)
    mk_ref[0] = mask.astype(mk_ref.dtype)
    cnt_ref[0, 0] = jnp.sum(mask, axis=0, keepdims=True)


def _odd_out(yc, yd, x, mod, w_out, gain, rw_hi, rw_lo, rb):
    b, s, d = x.shape
    rows = min(MIX_ROWS, s)
    row_spec = pl.BlockSpec((1, rows, d), lambda i, j: (i, j, 0))
    mix_spec = pl.BlockSpec((1, rows, MIX_W), lambda i, j: (i, j, 0))
    lane_spec = pl.BlockSpec((1, rows, LANES), lambda i, j: (i, j, 0))
    small = [w_out, gain, rw_hi, rw_lo, rb]
    return pl.pallas_call(
        _odd_out_kernel,
        grid=(b, s // rows),
        in_specs=[mix_spec, mix_spec, row_spec, pl.BlockSpec((1, 6, d), lambda i, j: (i, 0, 0))]
                 + [_full(t.shape) for t in small],
        out_specs=[row_spec, row_spec, lane_spec, lane_spec,
                   pl.BlockSpec((1, 1, 1, LANES), lambda i, j: (i, j, 0, 0))],
        out_shape=[jax.ShapeDtypeStruct(x.shape, F32),
                   jax.ShapeDtypeStruct(x.shape, BF16),
                   jax.ShapeDtypeStruct((b, s, LANES), F32),
                   jax.ShapeDtypeStruct((b, s, LANES), BF16),
                   jax.ShapeDtypeStruct((b, s // rows, 1, LANES), F32)],
        compiler_params=_params("arbitrary", "arbitrary"),
        name="odd_out_router",
    )(yc, yd, x, mod, *small)


def _moe_kernel(cnt_ref, cume_ref, h2_ref, wt_ref, mk_ref, x2_ref, mod_ref, wg_ref, wu_ref, wd_ref, o_ref,
                posc_scr, posr_scr, xs_scr, acc_scr, posb_scr, wb_scr):
    tile = pl.program_id(0)
    e = pl.program_id(1)
    f = pl.program_id(2)
    rows = h2_ref.shape[0]
    ch = MOE_CHUNK
    big, med, small = MOE_BIG_ROWS, MOE_MED_ROWS, MOE_SMALL_ROWS
    count = cnt_ref[tile * N_EXPERTS + e]
    units = lax.shift_right_logical(count + (small - 1), small.bit_length() - 1)
    n_big = lax.shift_right_logical(units, 2)
    n_med = lax.shift_right_logical(units & 3, 1)
    n_small = units & 1
    med_base = n_big * big
    small_base = med_base + n_med * med

    def for_blocks(body):
        lax.fori_loop(0, n_big, lambda kk, c: body(pl.multiple_of(kk * big, big), big) or c, 0)
        lax.fori_loop(0, n_med, lambda kk, c: body(pl.multiple_of(med_base, med), med) or c, 0)
        lax.fori_loop(0, n_small, lambda kk, c: body(pl.multiple_of(small_base, small), small) or c, 0)

    @pl.when((e == 0) & (f == 0))
    def _positions():
        o_ref[...] = jnp.zeros(o_ref.shape, F32)
        r_i = lax.broadcasted_iota(jnp.int32, (ch, ch), 0)
        c_i = lax.broadcasted_iota(jnp.int32, (ch, ch), 1)
        lower = jnp.where(c_i < r_i, 1.0, 0.0).astype(BF16)
        upper = jnp.where(r_i < c_i, 1.0, 0.0).astype(BF16)
        eye = jnp.where(lax.broadcasted_iota(jnp.int32, (N_EXPERTS, LANES), 0)
                        == lax.broadcasted_iota(jnp.int32, (N_EXPERTS, LANES), 1), 1.0, 0.0).astype(BF16)
        carry_c = jnp.zeros((1, LANES), F32)
        carry_r = jnp.zeros((N_EXPERTS, 1), F32)
        for blk in range(rows // ch):
            sl = slice(blk * ch, (blk + 1) * ch)
            mb = mk_ref[sl, :]
            mbf = mb.astype(F32)
            posc_scr[sl, :] = jnp.where(mbf > 0.0, _dot(lower, mb) + carry_c, -1.0)
            carry_c = carry_c + jnp.sum(mbf, axis=0, keepdims=True)
            mbt = _dot_nt(eye, mb)
            posr_scr[:, sl] = jnp.where(mbt > 0.0, _dot(mbt.astype(BF16), upper) + carry_r, -1.0)
            carry_r = carry_r + jnp.sum(mbt, axis=1, keepdims=True)

    @pl.when(f == 0)
    def _gather():
        posr = posr_scr[pl.ds(e, 1), :]

        def body(off, m):
            r_i = lax.broadcasted_iota(jnp.int32, (m, rows), 0).astype(F32)
            onehot = jnp.where(posr - off.astype(F32) == r_i, 1.0, 0.0).astype(BF16)
            xs_scr[pl.ds(off, m), :] = _dot(onehot, h2_ref[...]).astype(xs_scr.dtype)
            acc_scr[pl.ds(off, m), :] = jnp.zeros((m, acc_scr.shape[1]), F32)

        for_blocks(body)

    def ffn_body(off, m):
        xk = xs_scr[pl.ds(off, m), :]
        g = _dot(xk, wg_ref[0, 0])
        act = (g * _sigmoid(g)) * _dot(xk, wu_ref[0, 0])
        acc_scr[pl.ds(off, m), :] += _dot(act.astype(BF16), wd_ref[0])

    for_blocks(ffn_body)

    @pl.when(f == pl.num_programs(2) - 1)
    def _scatter():
        sub = MOE_SCATTER_ROWS
        lane = lax.broadcasted_iota(jnp.int32, (sub, LANES), 1)
        lane_f = lane.astype(F32)
        for tb in range(rows // sub):
            sl = slice(tb * sub, (tb + 1) * sub)
            pos_e = jnp.sum(jnp.where(lane == e, posc_scr[sl, :], 0.0), axis=1, keepdims=True)
            w_e = jnp.sum(jnp.where(lane == e, wt_ref[sl, :], 0.0), axis=1, keepdims=True)
            posb_scr[sl, :] = jnp.broadcast_to(pos_e, (sub, LANES)) - lane_f
            wb_scr[sl, :] = jnp.broadcast_to(w_e, (sub, LANES))

        nsub = rows // sub
        cum_base = (tile * N_EXPERTS + e) * (nsub + 1)

        def body(off, m):
            y = acc_scr[pl.ds(off, m), :].astype(BF16)
            off_f = off.astype(F32)
            reps = acc_scr.shape[1] // LANES
            for tb in range(nsub):
                def add_sub_block(tb=tb):
                    sl = slice(tb * sub, (tb + 1) * sub)
                    rel = posb_scr[sl, :] - off_f
                    onehot = jnp.concatenate(
                        [jnp.where(rel == float(g * LANES), 1.0, 0.0) for g in range(m // LANES)],
                        axis=1).astype(BF16)
                    o_ref[sl, :] += jnp.concatenate([wb_scr[sl, :]] * reps, axis=1) * _dot(onehot, y)

                if m == big:
                    add_sub_block()
                else:
                    pl.when((cume_ref[cum_base + tb] < off + m) & (cume_ref[cum_base + tb + 1] > off))(add_sub_block)

        for_blocks(body)

        @pl.when(e == N_EXPERTS - 1)
        def _residual():
            o_ref[...] = x2_ref[...] + mod_ref[0, 5:6, :] * o_ref[...]


def _moe(h2, wts, mask, x2, mod, counts, cum_counts, wg, wu, wd, seq):
    t, d = h2.shape
    n_f, cols = wg.shape[1], wg.shape[3]
    rows = min(MOE_ROWS, seq)
    once = pl.Buffered(1)
    tile_spec = lambda width: pl.BlockSpec((rows, width), lambda i, e, f, n, c: (i, 0), pipeline_mode=once)
    grid_spec = pltpu.PrefetchScalarGridSpec(
        num_scalar_prefetch=2,
        grid=(t // rows, N_EXPERTS, n_f),
        in_specs=[tile_spec(d), tile_spec(LANES), tile_spec(LANES), tile_spec(d),
                  pl.BlockSpec((1, 6, d), lambda i, e, f, n, c: (i * rows // seq, 0, 0)),
                  pl.BlockSpec((1, 1, d, cols), lambda i, e, f, n, c: (e, f, 0, 0)),
                  pl.BlockSpec((1, 1, d, cols), lambda i, e, f, n, c: (e, f, 0, 0)),
                  pl.BlockSpec((1, cols, d), lambda i, e, f, n, c: (e, f, 0))],
        out_specs=pl.BlockSpec((rows, d), lambda i, e, f, n, c: (i, 0), pipeline_mode=once),
        scratch_shapes=[pltpu.VMEM((rows, LANES), F32),
                        pltpu.VMEM((N_EXPERTS, rows), F32),
                        pltpu.VMEM((rows, d), BF16),
                        pltpu.VMEM((rows, d), F32),
                        pltpu.VMEM((rows, LANES), F32),
                        pltpu.VMEM((rows, LANES), F32)],
    )
    return pl.pallas_call(
        _moe_kernel,
        grid_spec=grid_spec,
        out_shape=jax.ShapeDtypeStruct((t, d), F32),
        compiler_params=_params("arbitrary", "arbitrary", "arbitrary"),
        name="moe_experts",
    )(counts, cum_counts, h2, wts, mask, x2, mod, wg, wu, wd)


def _block_diag(w):
    h, i, j = w.shape
    eye = jnp.eye(h, dtype=w.dtype)
    return jnp.einsum('hij,hk->hikj', w, eye).reshape(h * i, h * j)


def _rope_tables(seq):
    half = GROUP_DIM // 2
    inv = ROPE_THETA ** (-jnp.arange(half, dtype=F32) / half)
    ang = jnp.arange(seq).astype(F32)[:, None] * inv[None, :]
    cos, sin, zero = jnp.cos(ang), jnp.sin(ang), jnp.zeros_like(ang)
    reps = LANES // GROUP_DIM
    return (jnp.tile(jnp.concatenate([cos, cos], axis=1), (1, reps)),
            jnp.tile(jnp.concatenate([-sin, zero], axis=1), (1, reps)),
            jnp.tile(jnp.concatenate([zero, sin], axis=1), (1, reps)))


def _row(v):
    return v.reshape(1, -1).astype(F32)


def kernel(x, c, e_ada_w, e_ada_b, e_norm_mix, e_norm_ffn, e_w_in, e_conv_a_w, e_conv_a_b,
           e_ln_a_g, e_ln_a_b, e_conv_b_w, e_conv_b_b, e_lru_wa, e_lru_ba, e_lru_wx, e_lru_bx,
           e_lru_lambda, e_w_out, e_ffn_wg, e_ffn_wu, e_ffn_wd,
           o_ada_w, o_ada_b, o_norm_mix, o_norm_ffn, o_w_in, o_conv_c_w, o_q_norm, o_k_norm,
           o_w_out, o_router_w, o_router_b, o_moe_wg, o_moe_wu, o_moe_wd):
    b, s, d = x.shape
    assert s % MOBA_BLOCK == 0 and d % LANES == 0
    c_pad = jnp.zeros((SUBLANES, d), F32).at[:b].set(c.astype(F32))
    seg = jnp.kron(jnp.eye(N_GROUPS, dtype=F32), jnp.ones((GROUP_DIM, GROUP_DIM), F32)).astype(BF16)
    cos, sin_a, sin_b = _rope_tables(s)

    def modulation(w, bias):
        return _modulation(c_pad, w, _row(bias))[:b].reshape(b, 6, d)

    x = x.astype(F32)
    for layer in range(DEPTH):
        j = layer // 2
        if layer % 2 == 0:
            mod = modulation(e_ada_w[j], e_ada_b[j])
            x = _even_mixer(
                x, mod, _row(e_norm_mix[j]), e_w_in[j].astype(BF16),
                e_conv_a_w[j].astype(F32), _row(e_conv_a_b[j]), _row(e_ln_a_g[j]), _row(e_ln_a_b[j]),
                e_conv_b_w[j].astype(F32), _row(e_conv_b_b[j]),
                _block_diag(e_lru_wa[j]).astype(BF16), _row(e_lru_ba[j]),
                _block_diag(e_lru_wx[j]).astype(BF16), _row(e_lru_bx[j]),
                _row(e_lru_lambda[j]), e_w_out[j].astype(BF16))
            x = _dense_ffn(x, mod, _row(e_norm_ffn[j]), e_ffn_wg[j].astype(BF16),
                           e_ffn_wu[j].astype(BF16), e_ffn_wd[j].astype(BF16))
        else:
            mod = modulation(o_ada_w[j], o_ada_b[j])
            yc, q, k, v, kmean = _odd_mixer(
                x, mod, _row(o_norm_mix[j]), o_w_in[j].astype(BF16), o_conv_c_w[j].astype(F32),
                _row(jnp.tile(o_q_norm[j], N_GROUPS)), _row(jnp.tile(o_k_norm[j], N_GROUPS)),
                seg, cos, sin_a, sin_b)
            yd = _moba(q, k, v, kmean)
            rw = jnp.zeros((d, LANES), F32).at[:, :N_EXPERTS].set(o_router_w[j].astype(F32))
            rw_hi = rw.astype(BF16)
            rw_lo = (rw - rw_hi.astype(F32)).astype(BF16)
            rb = jnp.zeros((1, LANES), F32).at[0, :N_EXPERTS].set(o_router_b[j].astype(F32))
            x2, h2, wts, mask, cnt = _odd_out(yc, yd, x, mod, o_w_out[j].astype(BF16),
                                              _row(o_norm_ffn[j]), rw_hi, rw_lo, rb)
            rows = min(MOE_ROWS, s)
            assert MIX_ROWS == MOE_SCATTER_ROWS or s < MIX_ROWS
            sub_cnt = cnt.reshape(b * s // rows, -1, LANES)[:, :, :N_EXPERTS].astype(jnp.int32)
            cum = jnp.cumsum(sub_cnt, axis=1)
            cum = jnp.concatenate([jnp.zeros_like(cum[:, :1]), cum], axis=1)
            counts = cum[:, -1, :].reshape(-1)
            cum_counts = jnp.transpose(cum, (0, 2, 1)).reshape(-1)
            fdim = o_moe_wg.shape[-1]
            cols = MOE_COLS if fdim % MOE_COLS == 0 else fdim

            def chunked(w):
                return jnp.transpose(w.astype(BF16).reshape(N_EXPERTS, d, fdim // cols, cols), (0, 2, 1, 3))

            out = _moe(h2.reshape(b * s, d), wts.reshape(b * s, LANES), mask.reshape(b * s, LANES),
                       x2.reshape(b * s, d), mod, counts, cum_counts,
                       chunked(o_moe_wg[j]), chunked(o_moe_wu[j]), o_moe_wd[j].astype(BF16), s)
            x = out.reshape(b, s, d)
    return x
```

```python
import functools

import jax
import jax.numpy as jnp
from jax import lax
from jax.experimental import pallas as pl
from jax.experimental.pallas import tpu as pltpu

F32 = jnp.float32
BF16 = jnp.bfloat16

N_GROUPS = 8
GROUP_DIM = 64
MIX_W = N_GROUPS * GROUP_DIM
CONF_WIDTH = 31
LRU_CONV_WIDTH = 4
LRU_C = 8.0
SHORT_CONV_WIDTH = 3
MOBA_BLOCK = 256
MOBA_TOPK = 3
ROPE_THETA = 10000.0
N_EXPERTS = 8
EPS = 1e-6
DEPTH = 2

LANES = 128
SUBLANES = 8
VMEM_LIMIT_BYTES = 56 * 1024 * 1024

MIX_ROWS = 512
FFN_ROWS = 512
FFN_COLS = 2816
MOE_ROWS = 2048
MOE_COLS = 512
MOE_CHUNK = 256
MOE_BIG_ROWS = 512
MOE_MED_ROWS = 256
MOE_SMALL_ROWS = 128
MOE_SCATTER_ROWS = 512
MOE_WINDOW_ROWS = 256
MOBA_KEY_GROUP = 2
CONV_A_HALO = 32
CONV_S_HALO = 8


def _dot(a, b):
    return jnp.dot(a, b, preferred_element_type=F32)


def _dot_nt(a, b):
    return lax.dot_general(a, b, (((1,), (1,)), ((), ())), preferred_element_type=F32)


def _sigmoid(x):
    return 1.0 / (1.0 + jnp.exp(-x))


def _rms_mod(x, gain, scale, shift):
    ms = jnp.mean(x * x, axis=-1, keepdims=True)
    return (x * lax.rsqrt(ms + EPS)) * gain * (1.0 + scale) + shift


def _params(*sem):
    return pltpu.CompilerParams(dimension_semantics=sem, vmem_limit_bytes=VMEM_LIMIT_BYTES)


def _full(shape):
    n = len(shape)
    return pl.BlockSpec(shape, lambda *_: (0,) * n)


def _mod_kernel(c_ref, w_ref, b_ref, o_ref):
    c = c_ref[...]
    c_act = c * _sigmoid(c)
    o_ref[...] = _dot(c_act.astype(BF16), w_ref[...].astype(BF16)) + b_ref[...]


def _modulation(c_pad, w, b):
    d, n = w.shape
    tn = n // 4
    return pl.pallas_call(
        _mod_kernel,
        grid=(4,),
        in_specs=[_full(c_pad.shape),
                  pl.BlockSpec((d, tn), lambda j: (0, j)),
                  pl.BlockSpec((1, tn), lambda j: (0, j))],
        out_specs=pl.BlockSpec((c_pad.shape[0], tn), lambda j: (0, j)),
        out_shape=jax.ShapeDtypeStruct((c_pad.shape[0], n), F32),
        compiler_params=_params("arbitrary"),
        name="adaln_mod",
    )(c_pad, w, b)


def _causal_conv(buf, halo, rows, w_ref, width):
    total = halo + rows
    full = buf[0:total, :]
    acc = None
    for phase in range(SUBLANES):
        taps = [j for j in range(width) if (halo - (width - 1) + j) % SUBLANES == phase]
        if not taps:
            continue
        shifted = full if phase == 0 else pltpu.roll(full, total - phase, 0)
        for j in taps:
            start = halo - (width - 1) + j - phase
            term = w_ref[j:j + 1, :] * shifted[start:start + rows, :]
            acc = term if acc is None else acc + term
    return acc


def _even_mixer_kernel(x_ref, mod_ref, gain_ref, win_ref, caw_ref, cab_ref, lng_ref, lnb_ref,
                       cbw_ref, cbb_ref, wa_ref, ba_ref, wx_ref, bx_ref, lam_ref, wout_ref,
                       o_ref, abuf, bbuf, hcar):
    rows = x_ref.shape[1]

    @pl.when(pl.program_id(1) == 0)
    def _():
        abuf[0:CONV_A_HALO, :] = jnp.zeros((CONV_A_HALO, MIX_W), F32)
        bbuf[0:CONV_S_HALO, :] = jnp.zeros((CONV_S_HALO, MIX_W), F32)
        hcar[...] = jnp.zeros(hcar.shape, F32)

    x = x_ref[0]
    h = _rms_mod(x, gain_ref[...], mod_ref[0, 1:2, :], mod_ref[0, 0:1, :])
    u = _dot(h.astype(BF16), win_ref[...])
    a_val = u[:, 0 * MIX_W:1 * MIX_W]
    a_gate = u[:, 1 * MIX_W:2 * MIX_W]
    b_x = u[:, 2 * MIX_W:3 * MIX_W]
    b_gate = u[:, 3 * MIX_W:4 * MIX_W]

    abuf[CONV_A_HALO:CONV_A_HALO + rows, :] = a_val * _sigmoid(a_gate)
    ya = _causal_conv(abuf, CONV_A_HALO, rows, caw_ref, CONF_WIDTH) + cab_ref[...]
    abuf[0:CONV_A_HALO, :] = abuf[rows:rows + CONV_A_HALO, :]
    mu = jnp.mean(ya, axis=-1, keepdims=True)
    dev = ya - mu
    var = jnp.mean(dev * dev, axis=-1, keepdims=True)
    ya = dev * lax.rsqrt(var + EPS) * lng_ref[...] + lnb_ref[...]
    ya = ya * _sigmoid(ya)

    bbuf[CONV_S_HALO:CONV_S_HALO + rows, :] = b_x
    xb = _causal_conv(bbuf, CONV_S_HALO, rows, cbw_ref, LRU_CONV_WIDTH) + cbb_ref[...]
    bbuf[0:CONV_S_HALO, :] = bbuf[rows:rows + CONV_S_HALO, :]
    xb16 = xb.astype(BF16)
    r_gate = _sigmoid(_dot(xb16, wa_ref[...]) + ba_ref[...])
    i_gate = _sigmoid(_dot(xb16, wx_ref[...]) + bx_ref[...])
    lam = lam_ref[...]
    log_sig = -(jnp.maximum(-lam, 0.0) + jnp.log1p(jnp.exp(-jnp.abs(lam))))
    log_a = LRU_C * r_gate * log_sig
    a = jnp.exp(log_a)
    mult = jnp.sqrt(-jnp.tanh(log_a) * (1.0 + a * a))
    bterm = mult * (i_gate * xb)

    row = lax.broadcasted_iota(jnp.int32, (rows, MIX_W), 0)
    d = 1
    while d < rows:
        keep = row >= d
        a_prev = jnp.where(keep, pltpu.roll(a, d, 0), 1.0)
        b_prev = jnp.where(keep, pltpu.roll(bterm, d, 0), 0.0)
        bterm = bterm + a * b_prev
        a = a * a_prev
        d *= 2
    hseq = bterm + a * hcar[0:1, :]
    hcar[...] = jnp.broadcast_to(hseq[rows - 1:rows, :], hcar.shape)

    gelu = 0.5 * b_gate * (1.0 + jnp.tanh(0.7978845608028654 * (b_gate + 0.044715 * (b_gate * b_gate * b_gate))))
    yb = hseq * gelu

    mix = _dot(ya.astype(BF16), wout_ref[0:MIX_W, :]) + _dot(yb.astype(BF16), wout_ref[MIX_W:2 * MIX_W, :])
    o_ref[0] = x + mod_ref[0, 2:3, :] * mix


def _even_mixer(x, mod, gain, w_in, caw, cab, lng, lnb, cbw, cbb, wa, ba, wx, bx, lam, w_out):
    b, s, d = x.shape
    rows = min(MIX_ROWS, s)
    row_spec = pl.BlockSpec((1, rows, d), lambda i, j: (i, j, 0))
    small = [gain, w_in, caw, cab, lng, lnb, cbw, cbb, wa, ba, wx, bx, lam, w_out]
    return pl.pallas_call(
        _even_mixer_kernel,
        grid=(b, s // rows),
        in_specs=[row_spec, pl.BlockSpec((1, 6, d), lambda i, j: (i, 0, 0))] + [_full(t.shape) for t in small],
        out_specs=row_spec,
        out_shape=jax.ShapeDtypeStruct(x.shape, F32),
        scratch_shapes=[pltpu.VMEM((CONV_A_HALO + rows, MIX_W), F32),
                        pltpu.VMEM((CONV_S_HALO + rows, MIX_W), F32),
                        pltpu.VMEM((SUBLANES, MIX_W), F32)],
        compiler_params=_params("arbitrary", "arbitrary"),
        name="even_mixer",
    )(x, mod, *small)


def _ffn_kernel(x_ref, mod_ref, gain_ref, wg_ref, wu_ref, wd_ref, o_ref, h_scr):
    @pl.when(pl.program_id(2) == 0)
    def _():
        x = x_ref[0]
        h_scr[...] = _rms_mod(x, gain_ref[...], mod_ref[0, 4:5, :], mod_ref[0, 3:4, :]).astype(BF16)
        o_ref[0] = x

    h = h_scr[...]
    g = _dot(h, wg_ref[...])
    act = (g * _sigmoid(g)) * _dot(h, wu_ref[...])
    o_ref[0] += mod_ref[0, 5:6, :] * _dot(act.astype(BF16), wd_ref[...])


def _dense_ffn(x, mod, gain, wg, wu, wd):
    b, s, d = x.shape
    f = wg.shape[1]
    rows = min(FFN_ROWS, s)
    cols = FFN_COLS if f % FFN_COLS == 0 else f
    row_spec = pl.BlockSpec((1, rows, d), lambda i, j, k: (i, j, 0))
    return pl.pallas_call(
        _ffn_kernel,
        grid=(b, s // rows, f // cols),
        in_specs=[row_spec,
                  pl.BlockSpec((1, 6, d), lambda i, j, k: (i, 0, 0)),
                  _full(gain.shape),
                  pl.BlockSpec((d, cols), lambda i, j, k: (0, k)),
                  pl.BlockSpec((d, cols), lambda i, j, k: (0, k)),
                  pl.BlockSpec((cols, d), lambda i, j, k: (k, 0))],
        out_specs=row_spec,
        out_shape=jax.ShapeDtypeStruct(x.shape, F32),
        scratch_shapes=[pltpu.VMEM((rows, d), BF16)],
        compiler_params=_params("arbitrary", "arbitrary", "arbitrary"),
        name="dense_ffn",
    )(x, mod, gain, wg, wu, wd)


def _odd_mixer_kernel(x_ref, mod_ref, gain_ref, win_ref, ccw_ref, qn_ref, kn_ref, seg_ref,
                      cos_ref, sa_ref, sb_ref, yc_ref, q_ref, k_ref, v_ref, km_ref, cbuf):
    rows = x_ref.shape[1]

    @pl.when(pl.program_id(1) == 0)
    def _():
        cbuf[0:CONV_S_HALO, :] = jnp.zeros((CONV_S_HALO, MIX_W), F32)

    x = x_ref[0]
    h = _rms_mod(x, gain_ref[...], mod_ref[0, 1:2, :], mod_ref[0, 0:1, :])
    u = _dot(h.astype(BF16), win_ref[...])
    c_h = u[:, 0 * MIX_W:1 * MIX_W]
    c_b = u[:, 1 * MIX_W:2 * MIX_W]
    c_c = u[:, 2 * MIX_W:3 * MIX_W]
    q = u[:, 3 * MIX_W:4 * MIX_W]
    k = u[:, 4 * MIX_W:5 * MIX_W]
    v = u[:, 5 * MIX_W:6 * MIX_W]

    cbuf[CONV_S_HALO:CONV_S_HALO + rows, :] = c_c * c_h
    conv = _causal_conv(cbuf, CONV_S_HALO, rows, ccw_ref, SHORT_CONV_WIDTH)
    cbuf[0:CONV_S_HALO, :] = cbuf[rows:rows + CONV_S_HALO, :]
    yc_ref[0] = (c_b * conv).astype(yc_ref.dtype)

    seg = seg_ref[...]
    cos = cos_ref[...]
    sin_a = sa_ref[...]
    sin_b = sb_ref[...]

    def head_norm_rope(t, gain):
        t2 = t * t
        hi = t2.astype(BF16)
        lo = (t2 - hi.astype(F32)).astype(BF16)
        ssq = _dot(hi, seg) + _dot(lo, seg)
        t = t * lax.rsqrt(ssq * (1.0 / GROUP_DIM) + EPS) * gain
        outs = []
        for g in range(MIX_W // LANES):
            tg = t[:, g * LANES:(g + 1) * LANES]
            outs.append(tg * cos + pltpu.roll(tg, LANES - GROUP_DIM // 2, 1) * sin_a
                        + pltpu.roll(tg, GROUP_DIM // 2, 1) * sin_b)
        return jnp.concatenate(outs, axis=1)

    qr = head_norm_rope(q, qn_ref[...])
    kr = head_norm_rope(k, kn_ref[...])
    q_ref[0] = (qr * (GROUP_DIM ** -0.5)).astype(q_ref.dtype)
    k_ref[0] = kr.astype(k_ref.dtype)
    v_ref[0] = v.astype(v_ref.dtype)
    for i in range(rows // MOBA_BLOCK):
        km_ref[0, i] = jnp.mean(kr[i * MOBA_BLOCK:(i + 1) * MOBA_BLOCK, :], axis=0, keepdims=True)


def _odd_mixer(x, mod, gain, w_in, ccw, qn, kn, seg, cos, sin_a, sin_b):
    b, s, d = x.shape
    rows = min(MIX_ROWS, s)
    nb = s // MOBA_BLOCK
    row_spec = pl.BlockSpec((1, rows, d), lambda i, j: (i, j, 0))
    mix_spec = pl.BlockSpec((1, rows, MIX_W), lambda i, j: (i, j, 0))
    tab_spec = pl.BlockSpec((rows, LANES), lambda i, j: (j, 0))
    small = [gain, w_in, ccw, qn, kn, seg]
    mix_shape = jax.ShapeDtypeStruct((b, s, MIX_W), BF16)
    return pl.pallas_call(
        _odd_mixer_kernel,
        grid=(b, s // rows),
        in_specs=[row_spec, pl.BlockSpec((1, 6, d), lambda i, j: (i, 0, 0))]
                 + [_full(t.shape) for t in small] + [tab_spec, tab_spec, tab_spec],
        out_specs=[mix_spec, mix_spec, mix_spec, mix_spec,
                   pl.BlockSpec((1, rows // MOBA_BLOCK, 1, MIX_W), lambda i, j: (i, j, 0, 0))],
        out_shape=[mix_shape, mix_shape, mix_shape, mix_shape,
                   jax.ShapeDtypeStruct((b, nb, 1, MIX_W), F32)],
        scratch_shapes=[pltpu.VMEM((CONV_S_HALO + rows, MIX_W), F32)],
        compiler_params=_params("arbitrary", "arbitrary"),
        name="odd_mixer",
    )(x, mod, *small, cos, sin_a, sin_b)


def _top_blocks(gate, own):
    nblk = gate.shape[0]
    blk = lax.broadcasted_iota(jnp.int32, gate.shape, 0)
    remaining = blk < own
    sel = jnp.zeros(gate.shape, F32)
    for _ in range(MOBA_TOPK):
        gm = jnp.where(remaining, gate, -jnp.inf)
        top = jnp.max(gm, axis=0, keepdims=True)
        cand = remaining & (gm == top)
        first = jnp.min(jnp.where(cand, blk, nblk), axis=0, keepdims=True)
        pick = cand & (blk == first)
        sel = jnp.where(pick, 1.0, sel)
        remaining = remaining & jnp.logical_not(pick)
    return sel


def _moba_kernel(q_ref, k_ref, v_ref, km_ref, o_ref):
    own = pl.program_id(2)
    nblk = km_ref.shape[1]
    blk_rows = MOBA_BLOCK
    q = q_ref[0]
    lane = lax.broadcasted_iota(jnp.int32, q.shape, 1)
    zero = jnp.zeros_like(q)
    kmean = km_ref[0, :, 0, :].astype(BF16)
    row = lax.broadcasted_iota(jnp.int32, (blk_rows, blk_rows), 0)
    col = lax.broadcasted_iota(jnp.int32, (blk_rows, blk_rows), 1)
    causal_bias = jnp.where(col <= row, 0.0, -jnp.inf)
    pad = jnp.zeros((LANES - nblk, blk_rows), F32)

    q_heads, bias_heads = [], []
    for hh in range(LANES // GROUP_DIM):
        in_head = (lane >= hh * GROUP_DIM) & (lane < (hh + 1) * GROUP_DIM)
        qh = jnp.where(in_head, q, zero)
        sel_t = _top_blocks(_dot_nt(kmean, qh), own)
        sel = jnp.transpose(jnp.concatenate([sel_t, pad], axis=0))
        q_heads.append(qh)
        bias_heads.append(jnp.where(sel > 0.0, 0.0, -jnp.inf))

    group = MOBA_KEY_GROUP
    for g in range(-(-nblk // group)):
        nk = min((g + 1) * group, nblk)

        @pl.when(own // group == g)
        def _(g=g, nk=nk):
            keys = k_ref[0, 0:nk * blk_rows, :]
            vals = v_ref[0, 0:nk * blk_rows, :]
            outs = []
            for qh, bias in zip(q_heads, bias_heads):
                s = _dot_nt(qh, keys)
                pieces = []
                for jb in range(nk):
                    blk_bias = bias[:, jb:jb + 1]
                    if jb >= g * group:
                        blk_bias = jnp.where(own == jb, causal_bias, blk_bias)
                    pieces.append(s[:, jb * blk_rows:(jb + 1) * blk_rows] + blk_bias)
                s = jnp.concatenate(pieces, axis=1)
                m = jnp.max(s, axis=1, keepdims=True)
                p = jnp.exp(s - m)
                l = jnp.sum(p, axis=1, keepdims=True)
                outs.append(_dot(p.astype(BF16), vals) / l)
            o_ref[0] = jnp.where(lane < GROUP_DIM, outs[0], outs[1]).astype(o_ref.dtype)


def _moba(q, k, v, kmean):
    b, s, _ = q.shape
    nb = s // MOBA_BLOCK
    q_spec = pl.BlockSpec((1, MOBA_BLOCK, LANES), lambda i, p, j: (i, j, p))
    kv_spec = pl.BlockSpec((1, s, LANES), lambda i, p, j: (i, 0, p))
    return pl.pallas_call(
        _moba_kernel,
        grid=(b, MIX_W // LANES, nb),
        in_specs=[q_spec, kv_spec, kv_spec,
                  pl.BlockSpec((1, nb, 1, LANES), lambda i, p, j: (i, 0, 0, p))],
        out_specs=q_spec,
        out_shape=jax.ShapeDtypeStruct(q.shape, BF16),
        compiler_params=_params("arbitrary", "arbitrary", "arbitrary"),
        name="moba_attention",
    )(q, k, v, kmean)


def _odd_out_kernel(yc_ref, yd_ref, x_ref, mod_ref, wout_ref, gain_ref, rwh_ref, rwl_ref, rb_ref,
                    x2_ref, h2_ref, wt_ref, mk_ref, cnt_ref):
    x = x_ref[0]
    mix = _dot(yc_ref[0], wout_ref[0:MIX_W, :]) + _dot(yd_ref[0], wout_ref[MIX_W:2 * MIX_W, :])
    x2 = x + mod_ref[0, 2:3, :] * mix
    x2_ref[0] = x2
    h = _rms_mod(x2, gain_ref[...], mod_ref[0, 4:5, :], mod_ref[0, 3:4, :])
    h_hi = h.astype(BF16)
    h2_ref[0] = h_hi.astype(h2_ref.dtype)

    h_lo = (h - h_hi.astype(F32)).astype(BF16)
    rwh = rwh_ref[...]
    logits = _dot(h_hi, rwh) + _dot(h_lo, rwh) + _dot(h_hi, rwl_ref[...]) + rb_ref[...]
    lane = lax.broadcasted_iota(jnp.int32, logits.shape, 1)
    valid = lane < N_EXPERTS
    lg = jnp.where(valid, logits, -jnp.inf)
    top1 = jnp.max(lg, axis=1, keepdims=True)
    idx1 = jnp.min(jnp.where(lg == top1, lane, LANES), axis=1, keepdims=True)
    pick1 = lane == idx1
    lg2 = jnp.where(pick1, -jnp.inf, lg)
    top2 = jnp.max(lg2, axis=1, keepdims=True)
    idx2 = jnp.min(jnp.where((lg2 == top2) & valid & jnp.logical_not(pick1), lane, LANES), axis=1, keepdims=True)
    pick2 = lane == idx2
    e2 = jnp.exp(top2 - top1)
    den = 1.0 + e2
    wt_ref[0] = jnp.where(pick1, 1.0 / den, 0.0) + jnp.where(pick2, e2 / den, 0.0)
    mask = jnp.where(pick1 | pick2, 1.0, 0.0)
    mk_ref[0] = mask.astype(mk_ref.dtype)
    cnt_ref[0, 0] = jnp.sum(mask, axis=0, keepdims=True)


def _odd_out(yc, yd, x, mod, w_out, gain, rw_hi, rw_lo, rb):
    b, s, d = x.shape
    rows = min(MIX_ROWS, s)
    row_spec = pl.BlockSpec((1, rows, d), lambda i, j: (i, j, 0))
    mix_spec = pl.BlockSpec((1, rows, MIX_W), lambda i, j: (i, j, 0))
    lane_spec = pl.BlockSpec((1, rows, LANES), lambda i, j: (i, j, 0))
    small = [w_out, gain, rw_hi, rw_lo, rb]
    return pl.pallas_call(
        _odd_out_kernel,
        grid=(b, s // rows),
        in_specs=[mix_spec, mix_spec, row_spec, pl.BlockSpec((1, 6, d), lambda i, j: (i, 0, 0))]
                 + [_full(t.shape) for t in small],
        out_specs=[row_spec, row_spec, lane_spec, lane_spec,
                   pl.BlockSpec((1, 1, 1, LANES), lambda i, j: (i, j, 0, 0))],
        out_shape=[jax.ShapeDtypeStruct(x.shape, F32),
                   jax.ShapeDtypeStruct(x.shape, BF16),
                   jax.ShapeDtypeStruct((b, s, LANES), F32),
                   jax.ShapeDtypeStruct((b, s, LANES), BF16),
                   jax.ShapeDtypeStruct((b, s // rows, 1, LANES), F32)],
        compiler_params=_params("arbitrary", "arbitrary"),
        name="odd_out_router",
    )(yc, yd, x, mod, *small)


def _moe_kernel(cnt_ref, cume_ref, h2_ref, wt_ref, mk_ref, x2_ref, mod_ref, wg_ref, wu_ref, wd_ref, o_ref,
                posc_scr, posr_scr, xs_scr, acc_scr, posb_scr, wb_scr):
    tile = pl.program_id(0)
    e = pl.program_id(1)
    f = pl.program_id(2)
    rows = h2_ref.shape[0]
    ch = MOE_CHUNK
    big, med, small = MOE_BIG_ROWS, MOE_MED_ROWS, MOE_SMALL_ROWS
    count = cnt_ref[tile * N_EXPERTS + e]
    units = lax.shift_right_logical(count + (small - 1), small.bit_length() - 1)
    n_big = lax.shift_right_logical(units, 2)
    n_med = lax.shift_right_logical(units & 3, 1)
    n_small = units & 1
    med_base = n_big * big
    small_base = med_base + n_med * med

    def for_blocks(body):
        lax.fori_loop(0, n_big, lambda kk, c: body(pl.multiple_of(kk * big, big), big) or c, 0)
        lax.fori_loop(0, n_med, lambda kk, c: body(pl.multiple_of(med_base, med), med) or c, 0)
        lax.fori_loop(0, n_small, lambda kk, c: body(pl.multiple_of(small_base, small), small) or c, 0)

    @pl.when((e == 0) & (f == 0))
    def _positions():
        o_ref[...] = jnp.zeros(o_ref.shape, F32)
        r_i = lax.broadcasted_iota(jnp.int32, (ch, ch), 0)
        c_i = lax.broadcasted_iota(jnp.int32, (ch, ch), 1)
        lower = jnp.where(c_i < r_i, 1.0, 0.0).astype(BF16)
        upper = jnp.where(r_i < c_i, 1.0, 0.0).astype(BF16)
        eye = jnp.where(lax.broadcasted_iota(jnp.int32, (N_EXPERTS, LANES), 0)
                        == lax.broadcasted_iota(jnp.int32, (N_EXPERTS, LANES), 1), 1.0, 0.0).astype(BF16)
        carry_c = jnp.zeros((1, LANES), F32)
        carry_r = jnp.zeros((N_EXPERTS, 1), F32)
        for blk in range(rows // ch):
            sl = slice(blk * ch, (blk + 1) * ch)
            mb = mk_ref[sl, :]
            mbf = mb.astype(F32)
            posc_scr[sl, :] = jnp.where(mbf > 0.0, _dot(lower, mb) + carry_c, -1.0)
            carry_c = carry_c + jnp.sum(mbf, axis=0, keepdims=True)
            mbt = _dot_nt(eye, mb)
            posr_scr[:, sl] = jnp.where(mbt > 0.0, _dot(mbt.astype(BF16), upper) + carry_r, -1.0)
            carry_r = carry_r + jnp.sum(mbt, axis=1, keepdims=True)

    sub = MOE_SCATTER_ROWS
    nsub = rows // sub
    win = MOE_WINDOW_ROWS
    cum_base = (tile * N_EXPERTS + e) * (nsub + 1)
    width = acc_scr.shape[1]

    def windows(tb):
        lo = cume_ref[cum_base + tb]
        hi = cume_ref[cum_base + tb + 1]
        start = lax.shift_left(lax.shift_right_logical(lo, small.bit_length() - 1), small.bit_length() - 1)
        n_win = lax.shift_right_logical(hi - start + (win - 1), win.bit_length() - 1)
        return start, jnp.where(hi > lo, n_win, 0)

    def unit_rows(u):
        return pl.ds(pl.multiple_of(u * small, small), small)

    @pl.when(f == 0)
    def _gather():
        def zero(u, c):
            acc_scr[unit_rows(u), :] = jnp.zeros((small, width), F32)
            return c

        lax.fori_loop(0, units + win // small, zero, 0)
        r_i = lax.broadcasted_iota(jnp.int32, (win, sub), 0).astype(F32)
        for tb in range(nsub):
            start, n_win = windows(tb)
            posr = posr_scr[pl.ds(e, 1), tb * sub:(tb + 1) * sub]
            h2_tb = h2_ref[tb * sub:(tb + 1) * sub, :]

            def gather_window(k, c, start=start, posr=posr, h2_tb=h2_tb):
                off = pl.multiple_of(start + k * win, small)
                onehot = jnp.where(posr - off.astype(F32) == r_i, 1.0, 0.0).astype(BF16)
                acc_scr[pl.ds(off, win), :] += _dot(onehot, h2_tb)
                return c

            lax.fori_loop(0, n_win, gather_window, 0)

        def to_bf16(u, c):
            xs_scr[unit_rows(u), :] = acc_scr[unit_rows(u), :].astype(xs_scr.dtype)
            acc_scr[unit_rows(u), :] = jnp.zeros((small, width), F32)
            return c

        lax.fori_loop(0, units, to_bf16, 0)

    def ffn_body(off, m):
        xk = xs_scr[pl.ds(off, m), :]
        g = _dot(xk, wg_ref[0])
        act = (g * _sigmoid(g)) * _dot(xk, wu_ref[0])
        acc_scr[pl.ds(off, m), :] += _dot(act.astype(BF16), wd_ref[0])

    for_blocks(ffn_body)

    @pl.when(f == pl.num_programs(2) - 1)
    def _scatter():
        lane = lax.broadcasted_iota(jnp.int32, (sub, LANES), 1)
        lane_f = lane.astype(F32)
        reps = width // LANES
        for tb in range(nsub):
            sl = slice(tb * sub, (tb + 1) * sub)
            pos_e = jnp.sum(jnp.where(lane == e, posc_scr[sl, :], 0.0), axis=1, keepdims=True)
            w_e = jnp.sum(jnp.where(lane == e, wt_ref[sl, :], 0.0), axis=1, keepdims=True)
            posb_scr[sl, :] = jnp.broadcast_to(pos_e, (sub, LANES)) - lane_f
            wb_scr[sl, :] = jnp.broadcast_to(w_e, (sub, LANES))
            start, n_win = windows(tb)

            def scatter_window(k, c, start=start, sl=sl):
                off = pl.multiple_of(start + k * win, small)
                y = acc_scr[pl.ds(off, win), :].astype(BF16)
                rel = posb_scr[sl, :] - off.astype(F32)
                onehot = jnp.concatenate(
                    [jnp.where(rel == float(g * LANES), 1.0, 0.0) for g in range(win // LANES)], axis=1).astype(BF16)
                o_ref[sl, :] += jnp.concatenate([wb_scr[sl, :]] * reps, axis=1) * _dot(onehot, y)
                return c

            lax.fori_loop(0, n_win, scatter_window, 0)

        @pl.when(e == N_EXPERTS - 1)
        def _residual():
            o_ref[...] = x2_ref[...] + mod_ref[0, 5:6, :] * o_ref[...]


def _moe(h2, wts, mask, x2, mod, counts, cum_counts, wg, wu, wd, seq):
    t, d = h2.shape
    fdim = wg.shape[2]
    cols = MOE_COLS if fdim % MOE_COLS == 0 else fdim
    n_f = fdim // cols
    rows = min(MOE_ROWS, seq)
    once = pl.Buffered(1)
    tile_spec = lambda width: pl.BlockSpec((rows, width), lambda i, e, f, n, c: (i, 0), pipeline_mode=once)
    grid_spec = pltpu.PrefetchScalarGridSpec(
        num_scalar_prefetch=2,
        grid=(t // rows, N_EXPERTS, n_f),
        in_specs=[tile_spec(d), tile_spec(LANES), tile_spec(LANES), tile_spec(d),
                  pl.BlockSpec((1, 6, d), lambda i, e, f, n, c: (i * rows // seq, 0, 0)),
                  pl.BlockSpec((1, d, cols), lambda i, e, f, n, c: (e, 0, f)),
                  pl.BlockSpec((1, d, cols), lambda i, e, f, n, c: (e, 0, f)),
                  pl.BlockSpec((1, cols, d), lambda i, e, f, n, c: (e, f, 0))],
        out_specs=pl.BlockSpec((rows, d), lambda i, e, f, n, c: (i, 0), pipeline_mode=once),
        scratch_shapes=[pltpu.VMEM((rows, LANES), F32),
                        pltpu.VMEM((N_EXPERTS, rows), F32),
                        pltpu.VMEM((rows, d), BF16),
                        pltpu.VMEM((rows + MOE_WINDOW_ROWS, d), F32),
                        pltpu.VMEM((rows, LANES), F32),
                        pltpu.VMEM((rows, LANES), F32)],
    )
    return pl.pallas_call(
        _moe_kernel,
        grid_spec=grid_spec,
        out_shape=jax.ShapeDtypeStruct((t, d), F32),
        compiler_params=_params("arbitrary", "arbitrary", "arbitrary"),
        name="moe_experts",
    )(counts, cum_counts, h2, wts, mask, x2, mod, wg, wu, wd)


def _block_diag(w):
    h, i, j = w.shape
    eye = jnp.eye(h, dtype=w.dtype)
    return jnp.einsum('hij,hk->hikj', w, eye).reshape(h * i, h * j)


def _rope_tables(seq):
    half = GROUP_DIM // 2
    inv = ROPE_THETA ** (-jnp.arange(half, dtype=F32) / half)
    ang = jnp.arange(seq).astype(F32)[:, None] * inv[None, :]
    cos, sin, zero = jnp.cos(ang), jnp.sin(ang), jnp.zeros_like(ang)
    reps = LANES // GROUP_DIM
    return (jnp.tile(jnp.concatenate([cos, cos], axis=1), (1, reps)),
            jnp.tile(jnp.concatenate([-sin, zero], axis=1), (1, reps)),
            jnp.tile(jnp.concatenate([zero, sin], axis=1), (1, reps)))


def _row(v):
    return v.reshape(1, -1).astype(F32)


def kernel(x, c, e_ada_w, e_ada_b, e_norm_mix, e_norm_ffn, e_w_in, e_conv_a_w, e_conv_a_b,
           e_ln_a_g, e_ln_a_b, e_conv_b_w, e_conv_b_b, e_lru_wa, e_lru_ba, e_lru_wx, e_lru_bx,
           e_lru_lambda, e_w_out, e_ffn_wg, e_ffn_wu, e_ffn_wd,
           o_ada_w, o_ada_b, o_norm_mix, o_norm_ffn, o_w_in, o_conv_c_w, o_q_norm, o_k_norm,
           o_w_out, o_router_w, o_router_b, o_moe_wg, o_moe_wu, o_moe_wd):
    b, s, d = x.shape
    assert s % MOBA_BLOCK == 0 and d % LANES == 0
    c_pad = jnp.zeros((SUBLANES, d), F32).at[:b].set(c.astype(F32))
    seg = jnp.kron(jnp.eye(N_GROUPS, dtype=F32), jnp.ones((GROUP_DIM, GROUP_DIM), F32)).astype(BF16)
    cos, sin_a, sin_b = _rope_tables(s)

    def modulation(w, bias):
        return _modulation(c_pad, w, _row(bias))[:b].reshape(b, 6, d)

    x = x.astype(F32)
    for layer in range(DEPTH):
        j = layer // 2
        if layer % 2 == 0:
            mod = modulation(e_ada_w[j], e_ada_b[j])
            x = _even_mixer(
                x, mod, _row(e_norm_mix[j]), e_w_in[j].astype(BF16),
                e_conv_a_w[j].astype(F32), _row(e_conv_a_b[j]), _row(e_ln_a_g[j]), _row(e_ln_a_b[j]),
                e_conv_b_w[j].astype(F32), _row(e_conv_b_b[j]),
                _block_diag(e_lru_wa[j]).astype(BF16), _row(e_lru_ba[j]),
                _block_diag(e_lru_wx[j]).astype(BF16), _row(e_lru_bx[j]),
                _row(e_lru_lambda[j]), e_w_out[j].astype(BF16))
            x = _dense_ffn(x, mod, _row(e_norm_ffn[j]), e_ffn_wg[j].astype(BF16),
                           e_ffn_wu[j].astype(BF16), e_ffn_wd[j].astype(BF16))
        else:
            mod = modulation(o_ada_w[j], o_ada_b[j])
            yc, q, k, v, kmean = _odd_mixer(
                x, mod, _row(o_norm_mix[j]), o_w_in[j].astype(BF16), o_conv_c_w[j].astype(F32),
                _row(jnp.tile(o_q_norm[j], N_GROUPS)), _row(jnp.tile(o_k_norm[j], N_GROUPS)),
                seg, cos, sin_a, sin_b)
            yd = _moba(q, k, v, kmean)
            rw = jnp.zeros((d, LANES), F32).at[:, :N_EXPERTS].set(o_router_w[j].astype(F32))
            rw_hi = rw.astype(BF16)
            rw_lo = (rw - rw_hi.astype(F32)).astype(BF16)
            rb = jnp.zeros((1, LANES), F32).at[0, :N_EXPERTS].set(o_router_b[j].astype(F32))
            x2, h2, wts, mask, cnt = _odd_out(yc, yd, x, mod, o_w_out[j].astype(BF16),
                                              _row(o_norm_ffn[j]), rw_hi, rw_lo, rb)
            rows = min(MOE_ROWS, s)
            assert MIX_ROWS == MOE_SCATTER_ROWS or s < MIX_ROWS
            sub_cnt = cnt.reshape(b * s // rows, -1, LANES)[:, :, :N_EXPERTS].astype(jnp.int32)
            cum = jnp.cumsum(sub_cnt, axis=1)
            cum = jnp.concatenate([jnp.zeros_like(cum[:, :1]), cum], axis=1)
            counts = cum[:, -1, :].reshape(-1)
            cum_counts = jnp.transpose(cum, (0, 2, 1)).reshape(-1)
            out = _moe(h2.reshape(b * s, d), wts.reshape(b * s, LANES), mask.reshape(b * s, LANES),
                       x2.reshape(b * s, d), mod, counts, cum_counts,
                       o_moe_wg[j].astype(BF16), o_moe_wu[j].astype(BF16), o_moe_wd[j].astype(BF16), s)
            x = out.reshape(b, s, d)
    return x
```

```python
import functools

import jax
import jax.numpy as jnp
from jax import lax
from jax.experimental import pallas as pl
from jax.experimental.pallas import tpu as pltpu

F32 = jnp.float32
BF16 = jnp.bfloat16

N_GROUPS = 8
GROUP_DIM = 64
MIX_W = N_GROUPS * GROUP_DIM
CONF_WIDTH = 31
LRU_CONV_WIDTH = 4
LRU_C = 8.0
SHORT_CONV_WIDTH = 3
MOBA_BLOCK = 256
MOBA_TOPK = 3
ROPE_THETA = 10000.0
N_EXPERTS = 8
EPS = 1e-6
DEPTH = 2

LANES = 128
SUBLANES = 8
VMEM_LIMIT_BYTES = 56 * 1024 * 1024

MIX_ROWS = 512
FFN_ROWS = 512
FFN_COLS = 2816
MOE_ROWS = 2048
MOE_COLS = 512
MOE_CHUNK = 256
MOE_BIG_ROWS = 512
MOE_MED_ROWS = 256
MOE_SMALL_ROWS = 128
MOE_SCATTER_ROWS = 512
MOE_WINDOW_ROWS = 256
MOBA_KEY_GROUP = 2
CONV_A_HALO = 32
CONV_S_HALO = 8


def _dot(a, b):
    return jnp.dot(a, b, preferred_element_type=F32)


def _dot_nt(a, b):
    return lax.dot_general(a, b, (((1,), (1,)), ((), ())), preferred_element_type=F32)


def _sigmoid(x):
    return 1.0 / (1.0 + jnp.exp(-x))


def _rms_mod(x, gain, scale, shift):
    ms = jnp.mean(x * x, axis=-1, keepdims=True)
    return (x * lax.rsqrt(ms + EPS)) * gain * (1.0 + scale) + shift


def _params(*sem):
    return pltpu.CompilerParams(dimension_semantics=sem, vmem_limit_bytes=VMEM_LIMIT_BYTES)


def _full(shape):
    n = len(shape)
    return pl.BlockSpec(shape, lambda *_: (0,) * n)


def _mod_kernel(c_ref, w_ref, b_ref, o_ref):
    c = c_ref[...]
    c_act = c * _sigmoid(c)
    o_ref[...] = _dot(c_act.astype(BF16), w_ref[...].astype(BF16)) + b_ref[...]


def _modulation(c_pad, w, b):
    d, n = w.shape
    tn = n // 4
    return pl.pallas_call(
        _mod_kernel,
        grid=(4,),
        in_specs=[_full(c_pad.shape),
                  pl.BlockSpec((d, tn), lambda j: (0, j)),
                  pl.BlockSpec((1, tn), lambda j: (0, j))],
        out_specs=pl.BlockSpec((c_pad.shape[0], tn), lambda j: (0, j)),
        out_shape=jax.ShapeDtypeStruct((c_pad.shape[0], n), F32),
        compiler_params=_params("arbitrary"),
        name="adaln_mod",
    )(c_pad, w, b)


def _causal_conv(buf, halo, rows, w_ref, width):
    total = halo + rows
    full = buf[0:total, :]
    acc = None
    for phase in range(SUBLANES):
        taps = [j for j in range(width) if (halo - (width - 1) + j) % SUBLANES == phase]
        if not taps:
            continue
        shifted = full if phase == 0 else pltpu.roll(full, total - phase, 0)
        for j in taps:
            start = halo - (width - 1) + j - phase
            term = w_ref[j:j + 1, :] * shifted[start:start + rows, :]
            acc = term if acc is None else acc + term
    return acc


def _even_mixer_kernel(x_ref, mod_ref, gain_ref, win_ref, caw_ref, cab_ref, lng_ref, lnb_ref,
                       cbw_ref, cbb_ref, wa_ref, ba_ref, wx_ref, bx_ref, lam_ref, wout_ref,
                       o_ref, abuf, bbuf, hcar):
    rows = x_ref.shape[1]

    @pl.when(pl.program_id(1) == 0)
    def _():
        abuf[0:CONV_A_HALO, :] = jnp.zeros((CONV_A_HALO, MIX_W), F32)
        bbuf[0:CONV_S_HALO, :] = jnp.zeros((CONV_S_HALO, MIX_W), F32)
        hcar[...] = jnp.zeros(hcar.shape, F32)

    x = x_ref[0]
    h = _rms_mod(x, gain_ref[...], mod_ref[0, 1:2, :], mod_ref[0, 0:1, :])
    u = _dot(h.astype(BF16), win_ref[...])
    a_val = u[:, 0 * MIX_W:1 * MIX_W]
    a_gate = u[:, 1 * MIX_W:2 * MIX_W]
    b_x = u[:, 2 * MIX_W:3 * MIX_W]
    b_gate = u[:, 3 * MIX_W:4 * MIX_W]

    abuf[CONV_A_HALO:CONV_A_HALO + rows, :] = a_val * _sigmoid(a_gate)
    ya = _causal_conv(abuf, CONV_A_HALO, rows, caw_ref, CONF_WIDTH) + cab_ref[...]
    abuf[0:CONV_A_HALO, :] = abuf[rows:rows + CONV_A_HALO, :]
    mu = jnp.mean(ya, axis=-1, keepdims=True)
    dev = ya - mu
    var = jnp.mean(dev * dev, axis=-1, keepdims=True)
    ya = dev * lax.rsqrt(var + EPS) * lng_ref[...] + lnb_ref[...]
    ya = ya * _sigmoid(ya)

    bbuf[CONV_S_HALO:CONV_S_HALO + rows, :] = b_x
    xb = _causal_conv(bbuf, CONV_S_HALO, rows, cbw_ref, LRU_CONV_WIDTH) + cbb_ref[...]
    bbuf[0:CONV_S_HALO, :] = bbuf[rows:rows + CONV_S_HALO, :]
    xb16 = xb.astype(BF16)
    r_gate = _sigmoid(_dot(xb16, wa_ref[...]) + ba_ref[...])
    i_gate = _sigmoid(_dot(xb16, wx_ref[...]) + bx_ref[...])
    lam = lam_ref[...]
    log_sig = -(jnp.maximum(-lam, 0.0) + jnp.log1p(jnp.exp(-jnp.abs(lam))))
    log_a = LRU_C * r_gate * log_sig
    a = jnp.exp(log_a)
    mult = jnp.sqrt(-jnp.tanh(log_a) * (1.0 + a * a))
    bterm = mult * (i_gate * xb)

    row = lax.broadcasted_iota(jnp.int32, (rows, MIX_W), 0)
    d = 1
    while d < rows:
        keep = row >= d
        a_prev = jnp.where(keep, pltpu.roll(a, d, 0), 1.0)
        b_prev = jnp.where(keep, pltpu.roll(bterm, d, 0), 0.0)
        bterm = bterm + a * b_prev
        a = a * a_prev
        d *= 2
    hseq = bterm + a * hcar[0:1, :]
    hcar[...] = jnp.broadcast_to(hseq[rows - 1:rows, :], hcar.shape)

    gelu = 0.5 * b_gate * (1.0 + jnp.tanh(0.7978845608028654 * (b_gate + 0.044715 * (b_gate * b_gate * b_gate))))
    yb = hseq * gelu

    mix = _dot(ya.astype(BF16), wout_ref[0:MIX_W, :]) + _dot(yb.astype(BF16), wout_ref[MIX_W:2 * MIX_W, :])
    o_ref[0] = x + mod_ref[0, 2:3, :] * mix


def _even_mixer(x, mod, gain, w_in, caw, cab, lng, lnb, cbw, cbb, wa, ba, wx, bx, lam, w_out):
    b, s, d = x.shape
    rows = min(MIX_ROWS, s)
    row_spec = pl.BlockSpec((1, rows, d), lambda i, j: (i, j, 0))
    small = [gain, w_in, caw, cab, lng, lnb, cbw, cbb, wa, ba, wx, bx, lam, w_out]
    return pl.pallas_call(
        _even_mixer_kernel,
        grid=(b, s // rows),
        in_specs=[row_spec, pl.BlockSpec((1, 6, d), lambda i, j: (i, 0, 0))] + [_full(t.shape) for t in small],
        out_specs=row_spec,
        out_shape=jax.ShapeDtypeStruct(x.shape, F32),
        scratch_shapes=[pltpu.VMEM((CONV_A_HALO + rows, MIX_W), F32),
                        pltpu.VMEM((CONV_S_HALO + rows, MIX_W), F32),
                        pltpu.VMEM((SUBLANES, MIX_W), F32)],
        compiler_params=_params("arbitrary", "arbitrary"),
        name="even_mixer",
    )(x, mod, *small)


def _ffn_kernel(x_ref, mod_ref, gain_ref, wg_ref, wu_ref, wd_ref, o_ref, h_scr):
    @pl.when(pl.program_id(2) == 0)
    def _():
        x = x_ref[0]
        h_scr[...] = _rms_mod(x, gain_ref[...], mod_ref[0, 4:5, :], mod_ref[0, 3:4, :]).astype(BF16)
        o_ref[0] = x

    h = h_scr[...]
    g = _dot(h, wg_ref[...])
    act = (g * _sigmoid(g)) * _dot(h, wu_ref[...])
    o_ref[0] += mod_ref[0, 5:6, :] * _dot(act.astype(BF16), wd_ref[...])


def _dense_ffn(x, mod, gain, wg, wu, wd):
    b, s, d = x.shape
    f = wg.shape[1]
    rows = min(FFN_ROWS, s)
    cols = FFN_COLS if f % FFN_COLS == 0 else f
    row_spec = pl.BlockSpec((1, rows, d), lambda i, j, k: (i, j, 0))
    return pl.pallas_call(
        _ffn_kernel,
        grid=(b, s // rows, f // cols),
        in_specs=[row_spec,
                  pl.BlockSpec((1, 6, d), lambda i, j, k: (i, 0, 0)),
                  _full(gain.shape),
                  pl.BlockSpec((d, cols), lambda i, j, k: (0, k)),
                  pl.BlockSpec((d, cols), lambda i, j, k: (0, k)),
                  pl.BlockSpec((cols, d), lambda i, j, k: (k, 0))],
        out_specs=row_spec,
        out_shape=jax.ShapeDtypeStruct(x.shape, F32),
        scratch_shapes=[pltpu.VMEM((rows, d), BF16)],
        compiler_params=_params("arbitrary", "arbitrary", "arbitrary"),
        name="dense_ffn",
    )(x, mod, gain, wg, wu, wd)


def _odd_mixer_kernel(x_ref, mod_ref, gain_ref, win_ref, ccw_ref, qn_ref, kn_ref, seg_ref,
                      cos_ref, sa_ref, sb_ref, yc_ref, q_ref, k_ref, v_ref, km_ref, cbuf):
    rows = x_ref.shape[1]

    @pl.when(pl.program_id(1) == 0)
    def _():
        cbuf[0:CONV_S_HALO, :] = jnp.zeros((CONV_S_HALO, MIX_W), F32)

    x = x_ref[0]
    h = _rms_mod(x, gain_ref[...], mod_ref[0, 1:2, :], mod_ref[0, 0:1, :])
    u = _dot(h.astype(BF16), win_ref[...])
    c_h = u[:, 0 * MIX_W:1 * MIX_W]
    c_b = u[:, 1 * MIX_W:2 * MIX_W]
    c_c = u[:, 2 * MIX_W:3 * MIX_W]
    q = u[:, 3 * MIX_W:4 * MIX_W]
    k = u[:, 4 * MIX_W:5 * MIX_W]
    v = u[:, 5 * MIX_W:6 * MIX_W]

    cbuf[CONV_S_HALO:CONV_S_HALO + rows, :] = c_c * c_h
    conv = _causal_conv(cbuf, CONV_S_HALO, rows, ccw_ref, SHORT_CONV_WIDTH)
    cbuf[0:CONV_S_HALO, :] = cbuf[rows:rows + CONV_S_HALO, :]
    yc_ref[0] = (c_b * conv).astype(yc_ref.dtype)

    seg = seg_ref[...]
    cos = cos_ref[...]
    sin_a = sa_ref[...]
    sin_b = sb_ref[...]

    def head_norm_rope(t, gain):
        t2 = t * t
        hi = t2.astype(BF16)
        lo = (t2 - hi.astype(F32)).astype(BF16)
        ssq = _dot(hi, seg) + _dot(lo, seg)
        t = t * lax.rsqrt(ssq * (1.0 / GROUP_DIM) + EPS) * gain
        outs = []
        for g in range(MIX_W // LANES):
            tg = t[:, g * LANES:(g + 1) * LANES]
            outs.append(tg * cos + pltpu.roll(tg, LANES - GROUP_DIM // 2, 1) * sin_a
                        + pltpu.roll(tg, GROUP_DIM // 2, 1) * sin_b)
        return jnp.concatenate(outs, axis=1)

    qr = head_norm_rope(q, qn_ref[...])
    kr = head_norm_rope(k, kn_ref[...])
    q_ref[0] = (qr * (GROUP_DIM ** -0.5)).astype(q_ref.dtype)
    k_ref[0] = kr.astype(k_ref.dtype)
    v_ref[0] = v.astype(v_ref.dtype)
    for i in range(rows // MOBA_BLOCK):
        km_ref[0, i] = jnp.mean(kr[i * MOBA_BLOCK:(i + 1) * MOBA_BLOCK, :], axis=0, keepdims=True)


def _odd_mixer(x, mod, gain, w_in, ccw, qn, kn, seg, cos, sin_a, sin_b):
    b, s, d = x.shape
    rows = min(MIX_ROWS, s)
    nb = s // MOBA_BLOCK
    row_spec = pl.BlockSpec((1, rows, d), lambda i, j: (i, j, 0))
    mix_spec = pl.BlockSpec((1, rows, MIX_W), lambda i, j: (i, j, 0))
    tab_spec = pl.BlockSpec((rows, LANES), lambda i, j: (j, 0))
    small = [gain, w_in, ccw, qn, kn, seg]
    mix_shape = jax.ShapeDtypeStruct((b, s, MIX_W), BF16)
    return pl.pallas_call(
        _odd_mixer_kernel,
        grid=(b, s // rows),
        in_specs=[row_spec, pl.BlockSpec((1, 6, d), lambda i, j: (i, 0, 0))]
                 + [_full(t.shape) for t in small] + [tab_spec, tab_spec, tab_spec],
        out_specs=[mix_spec, mix_spec, mix_spec, mix_spec,
                   pl.BlockSpec((1, rows // MOBA_BLOCK, 1, MIX_W), lambda i, j: (i, j, 0, 0))],
        out_shape=[mix_shape, mix_shape, mix_shape, mix_shape,
                   jax.ShapeDtypeStruct((b, nb, 1, MIX_W), F32)],
        scratch_shapes=[pltpu.VMEM((CONV_S_HALO + rows, MIX_W), F32)],
        compiler_params=_params("arbitrary", "arbitrary"),
        name="odd_mixer",
    )(x, mod, *small, cos, sin_a, sin_b)


def _top_blocks(gate, own):
    nblk = gate.shape[0]
    blk = lax.broadcasted_iota(jnp.int32, gate.shape, 0)
    remaining = blk < own
    sel = jnp.zeros(gate.shape, F32)
    for _ in range(MOBA_TOPK):
        gm = jnp.where(remaining, gate, -jnp.inf)
        top = jnp.max(gm, axis=0, keepdims=True)
        cand = remaining & (gm == top)
        first = jnp.min(jnp.where(cand, blk, nblk), axis=0, keepdims=True)
        pick = cand & (blk == first)
        sel = jnp.where(pick, 1.0, sel)
        remaining = remaining & jnp.logical_not(pick)
    return sel


def _moba_kernel(q_ref, k_ref, v_ref, km_ref, *rest):
    n_cast = (len(rest) - 1) // 2
    o_ref = rest[n_cast]
    for src, dst in zip(rest[:n_cast], rest[n_cast + 1:]):
        dst[...] = src[...].astype(dst.dtype)
    own = pl.program_id(2)
    nblk = km_ref.shape[1]
    blk_rows = MOBA_BLOCK
    q = q_ref[0]
    lane = lax.broadcasted_iota(jnp.int32, q.shape, 1)
    zero = jnp.zeros_like(q)
    kmean = km_ref[0, :, 0, :].astype(BF16)
    row = lax.broadcasted_iota(jnp.int32, (blk_rows, blk_rows), 0)
    col = lax.broadcasted_iota(jnp.int32, (blk_rows, blk_rows), 1)
    causal_bias = jnp.where(col <= row, 0.0, -jnp.inf)
    pad = jnp.zeros((LANES - nblk, blk_rows), F32)

    q_heads, bias_heads = [], []
    for hh in range(LANES // GROUP_DIM):
        in_head = (lane >= hh * GROUP_DIM) & (lane < (hh + 1) * GROUP_DIM)
        qh = jnp.where(in_head, q, zero)
        sel_t = _top_blocks(_dot_nt(kmean, qh), own)
        sel = jnp.transpose(jnp.concatenate([sel_t, pad], axis=0))
        q_heads.append(qh)
        bias_heads.append(jnp.where(sel > 0.0, 0.0, -jnp.inf))

    group = MOBA_KEY_GROUP
    for g in range(-(-nblk // group)):
        nk = min((g + 1) * group, nblk)

        @pl.when(own // group == g)
        def _(g=g, nk=nk):
            keys = k_ref[0, 0:nk * blk_rows, :]
            vals = v_ref[0, 0:nk * blk_rows, :]
            outs = []
            for qh, bias in zip(q_heads, bias_heads):
                s = _dot_nt(qh, keys)
                pieces = []
                for jb in range(nk):
                    blk_bias = bias[:, jb:jb + 1]
                    if jb >= g * group:
                        blk_bias = jnp.where(own == jb, causal_bias, blk_bias)
                    pieces.append(s[:, jb * blk_rows:(jb + 1) * blk_rows] + blk_bias)
                s = jnp.concatenate(pieces, axis=1)
                m = jnp.max(s, axis=1, keepdims=True)
                p = jnp.exp(s - m)
                l = jnp.sum(p, axis=1, keepdims=True)
                outs.append(_dot(p.astype(BF16), vals) / l)
            o_ref[0] = jnp.where(lane < GROUP_DIM, outs[0], outs[1]).astype(o_ref.dtype)


def _moba(q, k, v, kmean, cast_weights):
    b, s, _ = q.shape
    nb = s // MOBA_BLOCK
    pairs = MIX_W // LANES
    steps = b * pairs * nb
    q_spec = pl.BlockSpec((1, MOBA_BLOCK, LANES), lambda i, p, j: (i, j, p))
    kv_spec = pl.BlockSpec((1, s, LANES), lambda i, p, j: (i, 0, p))
    bf16_rows = 2 * SUBLANES
    sliced = [w.reshape(steps, -1, w.shape[-1]) for w in cast_weights]
    assert all(w.shape[1] % bf16_rows == 0 and w.shape[2] % LANES == 0 for w in sliced)
    w_specs = [pl.BlockSpec((1,) + w.shape[1:], lambda i, p, j: ((i * pairs + p) * nb + j, 0, 0)) for w in sliced]
    outs = pl.pallas_call(
        _moba_kernel,
        grid=(b, pairs, nb),
        in_specs=[q_spec, kv_spec, kv_spec,
                  pl.BlockSpec((1, nb, 1, LANES), lambda i, p, j: (i, 0, 0, p))] + w_specs,
        out_specs=[q_spec] + w_specs,
        out_shape=[jax.ShapeDtypeStruct(q.shape, BF16)] + [jax.ShapeDtypeStruct(w.shape, BF16) for w in sliced],
        compiler_params=_params("arbitrary", "arbitrary", "arbitrary"),
        name="moba_attention",
    )(q, k, v, kmean, *sliced)
    return outs[0], [o.reshape(w.shape) for o, w in zip(outs[1:], cast_weights)]


def _odd_out_kernel(yc_ref, yd_ref, x_ref, mod_ref, wout_ref, gain_ref, rwh_ref, rwl_ref, rb_ref,
                    x2_ref, h2_ref, wt_ref, mk_ref, cnt_ref):
    x = x_ref[0]
    mix = _dot(yc_ref[0], wout_ref[0:MIX_W, :]) + _dot(yd_ref[0], wout_ref[MIX_W:2 * MIX_W, :])
    x2 = x + mod_ref[0, 2:3, :] * mix
    x2_ref[0] = x2
    h = _rms_mod(x2, gain_ref[...], mod_ref[0, 4:5, :], mod_ref[0, 3:4, :])
    h_hi = h.astype(BF16)
    h2_ref[0] = h_hi.astype(h2_ref.dtype)

    h_lo = (h - h_hi.astype(F32)).astype(BF16)
    rwh = rwh_ref[...]
    logits = _dot(h_hi, rwh) + _dot(h_lo, rwh) + _dot(h_hi, rwl_ref[...]) + rb_ref[...]
    lane = lax.broadcasted_iota(jnp.int32, logits.shape, 1)
    valid = lane < N_EXPERTS
    lg = jnp.where(valid, logits, -jnp.inf)
    top1 = jnp.max(lg, axis=1, keepdims=True)
    idx1 = jnp.min(jnp.where(lg == top1, lane, LANES), axis=1, keepdims=True)
    pick1 = lane == idx1
    lg2 = jnp.where(pick1, -jnp.inf, lg)
    top2 = jnp.max(lg2, axis=1, keepdims=True)
    idx2 = jnp.min(jnp.where((lg2 == top2) & valid & jnp.logical_not(pick1), lane, LANES), axis=1, keepdims=True)
    pick2 = lane == idx2
    e2 = jnp.exp(top2 - top1)
    den = 1.0 + e2
    wt_ref[0] = jnp.where(pick1, 1.0 / den, 0.0) + jnp.where(pick2, e2 / den, 0.0)
    mask = jnp.where(pick1 | pick2, 1.0, 0.0)
    mk_ref[0] = mask.astype(mk_ref.dtype)
    cnt_ref[0, 0] = jnp.sum(mask, axis=0, keepdims=True)


def _odd_out(yc, yd, x, mod, w_out, gain, rw_hi, rw_lo, rb):
    b, s, d = x.shape
    rows = min(MIX_ROWS, s)
    row_spec = pl.BlockSpec((1, rows, d), lambda i, j: (i, j, 0))
    mix_spec = pl.BlockSpec((1, rows, MIX_W), lambda i, j: (i, j, 0))
    lane_spec = pl.BlockSpec((1, rows, LANES), lambda i, j: (i, j, 0))
    small = [w_out, gain, rw_hi, rw_lo, rb]
    return pl.pallas_call(
        _odd_out_kernel,
        grid=(b, s // rows),
        in_specs=[mix_spec, mix_spec, row_spec, pl.BlockSpec((1, 6, d), lambda i, j: (i, 0, 0))]
                 + [_full(t.shape) for t in small],
        out_specs=[row_spec, row_spec, lane_spec, lane_spec,
                   pl.BlockSpec((1, 1, 1, LANES), lambda i, j: (i, j, 0, 0))],
        out_shape=[jax.ShapeDtypeStruct(x.shape, F32),
                   jax.ShapeDtypeStruct(x.shape, BF16),
                   jax.ShapeDtypeStruct((b, s, LANES), F32),
                   jax.ShapeDtypeStruct((b, s, LANES), BF16),
                   jax.ShapeDtypeStruct((b, s // rows, 1, LANES), F32)],
        compiler_params=_params("arbitrary", "arbitrary"),
        name="odd_out_router",
    )(yc, yd, x, mod, *small)


def _moe_kernel(cnt_ref, cume_ref, h2_ref, wt_ref, mk_ref, x2_ref, mod_ref, wg_ref, wu_ref, wd_ref, o_ref,
                posc_scr, posr_scr, xs_scr, acc_scr, posb_scr, wb_scr):
    tile = pl.program_id(0)
    e = pl.program_id(1)
    f = pl.program_id(2)
    rows = h2_ref.shape[0]
    ch = MOE_CHUNK
    big, med, small = MOE_BIG_ROWS, MOE_MED_ROWS, MOE_SMALL_ROWS
    count = cnt_ref[tile * N_EXPERTS + e]
    units = lax.shift_right_logical(count + (small - 1), small.bit_length() - 1)
    n_big = lax.shift_right_logical(units, 2)
    n_med = lax.shift_right_logical(units & 3, 1)
    n_small = units & 1
    med_base = n_big * big
    small_base = med_base + n_med * med

    def for_blocks(body):
        lax.fori_loop(0, n_big, lambda kk, c: body(pl.multiple_of(kk * big, big), big) or c, 0)
        lax.fori_loop(0, n_med, lambda kk, c: body(pl.multiple_of(med_base, med), med) or c, 0)
        lax.fori_loop(0, n_small, lambda kk, c: body(pl.multiple_of(small_base, small), small) or c, 0)

    @pl.when((e == 0) & (f == 0))
    def _positions():
        o_ref[...] = jnp.zeros(o_ref.shape, F32)
        r_i = lax.broadcasted_iota(jnp.int32, (ch, ch), 0)
        c_i = lax.broadcasted_iota(jnp.int32, (ch, ch), 1)
        lower = jnp.where(c_i < r_i, 1.0, 0.0).astype(BF16)
        upper = jnp.where(r_i < c_i, 1.0, 0.0).astype(BF16)
        eye = jnp.where(lax.broadcasted_iota(jnp.int32, (N_EXPERTS, LANES), 0)
                        == lax.broadcasted_iota(jnp.int32, (N_EXPERTS, LANES), 1), 1.0, 0.0).astype(BF16)
        carry_c = jnp.zeros((1, LANES), F32)
        carry_r = jnp.zeros((N_EXPERTS, 1), F32)
        for blk in range(rows // ch):
            sl = slice(blk * ch, (blk + 1) * ch)
            mb = mk_ref[sl, :]
            mbf = mb.astype(F32)
            posc_scr[sl, :] = jnp.where(mbf > 0.0, _dot(lower, mb) + carry_c, -1.0)
            carry_c = carry_c + jnp.sum(mbf, axis=0, keepdims=True)
            mbt = _dot_nt(eye, mb)
            posr_scr[:, sl] = jnp.where(mbt > 0.0, _dot(mbt.astype(BF16), upper) + carry_r, -1.0)
            carry_r = carry_r + jnp.sum(mbt, axis=1, keepdims=True)

    sub = MOE_SCATTER_ROWS
    nsub = rows // sub
    win = MOE_WINDOW_ROWS
    cum_base = (tile * N_EXPERTS + e) * (nsub + 1)
    width = acc_scr.shape[1]

    def windows(tb):
        lo = cume_ref[cum_base + tb]
        hi = cume_ref[cum_base + tb + 1]
        start = lax.shift_left(lax.shift_right_logical(lo, small.bit_length() - 1), small.bit_length() - 1)
        n_win = lax.shift_right_logical(hi - start + (win - 1), win.bit_length() - 1)
        return start, jnp.where(hi > lo, n_win, 0)

    def unit_rows(u):
        return pl.ds(pl.multiple_of(u * small, small), small)

    @pl.when(f == 0)
    def _gather():
        def zero(u, c):
            acc_scr[unit_rows(u), :] = jnp.zeros((small, width), F32)
            return c

        lax.fori_loop(0, units + win // small, zero, 0)
        r_i = lax.broadcasted_iota(jnp.int32, (win, sub), 0).astype(F32)
        for tb in range(nsub):
            start, n_win = windows(tb)
            posr = posr_scr[pl.ds(e, 1), tb * sub:(tb + 1) * sub]
            h2_tb = h2_ref[tb * sub:(tb + 1) * sub, :]

            def gather_window(k, c, start=start, posr=posr, h2_tb=h2_tb):
                off = pl.multiple_of(start + k * win, small)
                onehot = jnp.where(posr - off.astype(F32) == r_i, 1.0, 0.0).astype(BF16)
                acc_scr[pl.ds(off, win), :] += _dot(onehot, h2_tb)
                return c

            lax.fori_loop(0, n_win, gather_window, 0)

        def to_bf16(u, c):
            xs_scr[unit_rows(u), :] = acc_scr[unit_rows(u), :].astype(xs_scr.dtype)
            acc_scr[unit_rows(u), :] = jnp.zeros((small, width), F32)
            return c

        lax.fori_loop(0, units, to_bf16, 0)

    def ffn_body(off, m):
        xk = xs_scr[pl.ds(off, m), :]
        g = _dot(xk, wg_ref[0])
        act = (g * _sigmoid(g)) * _dot(xk, wu_ref[0])
        acc_scr[pl.ds(off, m), :] += _dot(act.astype(BF16), wd_ref[0])

    for_blocks(ffn_body)

    @pl.when(f == pl.num_programs(2) - 1)
    def _scatter():
        lane = lax.broadcasted_iota(jnp.int32, (sub, LANES), 1)
        lane_f = lane.astype(F32)
        reps = width // LANES
        for tb in range(nsub):
            sl = slice(tb * sub, (tb + 1) * sub)
            pos_e = jnp.sum(jnp.where(lane == e, posc_scr[sl, :], 0.0), axis=1, keepdims=True)
            w_e = jnp.sum(jnp.where(lane == e, wt_ref[sl, :], 0.0), axis=1, keepdims=True)
            posb_scr[sl, :] = jnp.broadcast_to(pos_e, (sub, LANES)) - lane_f
            wb_scr[sl, :] = jnp.broadcast_to(w_e, (sub, LANES))
            start, n_win = windows(tb)

            def scatter_window(k, c, start=start, sl=sl):
                off = pl.multiple_of(start + k * win, small)
                y = acc_scr[pl.ds(off, win), :].astype(BF16)
                rel = posb_scr[sl, :] - off.astype(F32)
                onehot = jnp.concatenate(
                    [jnp.where(rel == float(g * LANES), 1.0, 0.0) for g in range(win // LANES)], axis=1).astype(BF16)
                o_ref[sl, :] += jnp.concatenate([wb_scr[sl, :]] * reps, axis=1) * _dot(onehot, y)
                return c

            lax.fori_loop(0, n_win, scatter_window, 0)

        @pl.when(e == N_EXPERTS - 1)
        def _residual():
            o_ref[...] = x2_ref[...] + mod_ref[0, 5:6, :] * o_ref[...]


def _moe(h2, wts, mask, x2, mod, counts, cum_counts, wg, wu, wd, seq):
    t, d = h2.shape
    fdim = wg.shape[2]
    cols = MOE_COLS if fdim % MOE_COLS == 0 else fdim
    n_f = fdim // cols
    rows = min(MOE_ROWS, seq)
    once = pl.Buffered(1)
    tile_spec = lambda width: pl.BlockSpec((rows, width), lambda i, e, f, n, c: (i, 0), pipeline_mode=once)
    grid_spec = pltpu.PrefetchScalarGridSpec(
        num_scalar_prefetch=2,
        grid=(t // rows, N_EXPERTS, n_f),
        in_specs=[tile_spec(d), tile_spec(LANES), tile_spec(LANES), tile_spec(d),
                  pl.BlockSpec((1, 6, d), lambda i, e, f, n, c: (i * rows // seq, 0, 0)),
                  pl.BlockSpec((1, d, cols), lambda i, e, f, n, c: (e, 0, f)),
                  pl.BlockSpec((1, d, cols), lambda i, e, f, n, c: (e, 0, f)),
                  pl.BlockSpec((1, cols, d), lambda i, e, f, n, c: (e, f, 0))],
        out_specs=pl.BlockSpec((rows, d), lambda i, e, f, n, c: (i, 0), pipeline_mode=once),
        scratch_shapes=[pltpu.VMEM((rows, LANES), F32),
                        pltpu.VMEM((N_EXPERTS, rows), F32),
                        pltpu.VMEM((rows, d), BF16),
                        pltpu.VMEM((rows + MOE_WINDOW_ROWS, d), F32),
                        pltpu.VMEM((rows, LANES), F32),
                        pltpu.VMEM((rows, LANES), F32)],
    )
    return pl.pallas_call(
        _moe_kernel,
        grid_spec=grid_spec,
        out_shape=jax.ShapeDtypeStruct((t, d), F32),
        compiler_params=_params("arbitrary", "arbitrary", "arbitrary"),
        name="moe_experts",
    )(counts, cum_counts, h2, wts, mask, x2, mod, wg, wu, wd)


def _block_diag(w):
    h, i, j = w.shape
    eye = jnp.eye(h, dtype=w.dtype)
    return jnp.einsum('hij,hk->hikj', w, eye).reshape(h * i, h * j)


def _rope_tables(seq):
    half = GROUP_DIM // 2
    inv = ROPE_THETA ** (-jnp.arange(half, dtype=F32) / half)
    ang = jnp.arange(seq).astype(F32)[:, None] * inv[None, :]
    cos, sin, zero = jnp.cos(ang), jnp.sin(ang), jnp.zeros_like(ang)
    reps = LANES // GROUP_DIM
    return (jnp.tile(jnp.concatenate([cos, cos], axis=1), (1, reps)),
            jnp.tile(jnp.concatenate([-sin, zero], axis=1), (1, reps)),
            jnp.tile(jnp.concatenate([zero, sin], axis=1), (1, reps)))


def _row(v):
    return v.reshape(1, -1).astype(F32)


def kernel(x, c, e_ada_w, e_ada_b, e_norm_mix, e_norm_ffn, e_w_in, e_conv_a_w, e_conv_a_b,
           e_ln_a_g, e_ln_a_b, e_conv_b_w, e_conv_b_b, e_lru_wa, e_lru_ba, e_lru_wx, e_lru_bx,
           e_lru_lambda, e_w_out, e_ffn_wg, e_ffn_wu, e_ffn_wd,
           o_ada_w, o_ada_b, o_norm_mix, o_norm_ffn, o_w_in, o_conv_c_w, o_q_norm, o_k_norm,
           o_w_out, o_router_w, o_router_b, o_moe_wg, o_moe_wu, o_moe_wd):
    b, s, d = x.shape
    assert s % MOBA_BLOCK == 0 and d % LANES == 0
    c_pad = jnp.zeros((SUBLANES, d), F32).at[:b].set(c.astype(F32))
    seg = jnp.kron(jnp.eye(N_GROUPS, dtype=F32), jnp.ones((GROUP_DIM, GROUP_DIM), F32)).astype(BF16)
    cos, sin_a, sin_b = _rope_tables(s)

    def modulation(w, bias):
        return _modulation(c_pad, w, _row(bias))[:b].reshape(b, 6, d)

    x = x.astype(F32)
    for layer in range(DEPTH):
        j = layer // 2
        if layer % 2 == 0:
            mod = modulation(e_ada_w[j], e_ada_b[j])
            x = _even_mixer(
                x, mod, _row(e_norm_mix[j]), e_w_in[j].astype(BF16),
                e_conv_a_w[j].astype(F32), _row(e_conv_a_b[j]), _row(e_ln_a_g[j]), _row(e_ln_a_b[j]),
                e_conv_b_w[j].astype(F32), _row(e_conv_b_b[j]),
                _block_diag(e_lru_wa[j]).astype(BF16), _row(e_lru_ba[j]),
                _block_diag(e_lru_wx[j]).astype(BF16), _row(e_lru_bx[j]),
                _row(e_lru_lambda[j]), e_w_out[j].astype(BF16))
            x = _dense_ffn(x, mod, _row(e_norm_ffn[j]), e_ffn_wg[j].astype(BF16),
                           e_ffn_wu[j].astype(BF16), e_ffn_wd[j].astype(BF16))
        else:
            mod = modulation(o_ada_w[j], o_ada_b[j])
            yc, q, k, v, kmean = _odd_mixer(
                x, mod, _row(o_norm_mix[j]), o_w_in[j].astype(BF16), o_conv_c_w[j].astype(F32),
                _row(jnp.tile(o_q_norm[j], N_GROUPS)), _row(jnp.tile(o_k_norm[j], N_GROUPS)),
                seg, cos, sin_a, sin_b)
            yd, (moe_wg, moe_wu, moe_wd) = _moba(
                q, k, v, kmean, [o_moe_wg[j].astype(F32), o_moe_wu[j].astype(F32), o_moe_wd[j].astype(F32)])
            rw = jnp.zeros((d, LANES), F32).at[:, :N_EXPERTS].set(o_router_w[j].astype(F32))
            rw_hi = rw.astype(BF16)
            rw_lo = (rw - rw_hi.astype(F32)).astype(BF16)
            rb = jnp.zeros((1, LANES), F32).at[0, :N_EXPERTS].set(o_router_b[j].astype(F32))
            x2, h2, wts, mask, cnt = _odd_out(yc, yd, x, mod, o_w_out[j].astype(BF16),
                                              _row(o_norm_ffn[j]), rw_hi, rw_lo, rb)
            rows = min(MOE_ROWS, s)
            assert MIX_ROWS == MOE_SCATTER_ROWS or s < MIX_ROWS
            sub_cnt = cnt.reshape(b * s // rows, -1, LANES)[:, :, :N_EXPERTS].astype(jnp.int32)
            cum = jnp.cumsum(sub_cnt, axis=1)
            cum = jnp.concatenate([jnp.zeros_like(cum[:, :1]), cum], axis=1)
            counts = cum[:, -1, :].reshape(-1)
            cum_counts = jnp.transpose(cum, (0, 2, 1)).reshape(-1)
            out = _moe(h2.reshape(b * s, d), wts.reshape(b * s, LANES), mask.reshape(b * s, LANES),
                       x2.reshape(b * s, d), mod, counts, cum_counts,
                       moe_wg, moe_wu, moe_wd, s)
            x = out.reshape(b, s, d)
    return x
```

```python
import functools

import jax
import jax.numpy as jnp
from jax import lax
from jax.experimental import pallas as pl
from jax.experimental.pallas import tpu as pltpu

F32 = jnp.float32
BF16 = jnp.bfloat16

N_GROUPS = 8
GROUP_DIM = 64
MIX_W = N_GROUPS * GROUP_DIM
CONF_WIDTH = 31
LRU_CONV_WIDTH = 4
LRU_C = 8.0
SHORT_CONV_WIDTH = 3
MOBA_BLOCK = 256
MOBA_TOPK = 3
ROPE_THETA = 10000.0
N_EXPERTS = 8
EPS = 1e-6
DEPTH = 2

LANES = 128
SUBLANES = 8
VMEM_LIMIT_BYTES = 62 * 1024 * 1024

MIX_ROWS = 512
FFN_ROWS = 512
FFN_COLS = 2816
MOE_ROWS = 2048
MOE_COLS = 1792
MOE_CHUNK = 256
MOE_BIG_ROWS = 512
MOE_MED_ROWS = 256
MOE_SMALL_ROWS = 128
MOE_SCATTER_ROWS = 512
MOE_WINDOW_ROWS = 256
MOBA_KEY_GROUP = 2
CONV_A_HALO = 32
CONV_S_HALO = 8


def _dot(a, b):
    return jnp.dot(a, b, preferred_element_type=F32)


def _dot_nt(a, b):
    return lax.dot_general(a, b, (((1,), (1,)), ((), ())), preferred_element_type=F32)


def _sigmoid(x):
    return 1.0 / (1.0 + jnp.exp(-x))


def _rms_mod(x, gain, scale, shift):
    ms = jnp.mean(x * x, axis=-1, keepdims=True)
    return (x * lax.rsqrt(ms + EPS)) * gain * (1.0 + scale) + shift


def _params(*sem):
    return pltpu.CompilerParams(dimension_semantics=sem, vmem_limit_bytes=VMEM_LIMIT_BYTES)


def _full(shape):
    n = len(shape)
    return pl.BlockSpec(shape, lambda *_: (0,) * n)


def _mod_kernel(c_ref, w_ref, b_ref, o_ref):
    c = c_ref[...]
    c_act = c * _sigmoid(c)
    o_ref[...] = _dot(c_act.astype(BF16), w_ref[...].astype(BF16)) + b_ref[...]


def _modulation(c_pad, w, b):
    d, n = w.shape
    tn = n // 4
    return pl.pallas_call(
        _mod_kernel,
        grid=(4,),
        in_specs=[_full(c_pad.shape),
                  pl.BlockSpec((d, tn), lambda j: (0, j)),
                  pl.BlockSpec((1, tn), lambda j: (0, j))],
        out_specs=pl.BlockSpec((c_pad.shape[0], tn), lambda j: (0, j)),
        out_shape=jax.ShapeDtypeStruct((c_pad.shape[0], n), F32),
        compiler_params=_params("arbitrary"),
        name="adaln_mod",
    )(c_pad, w, b)


def _causal_conv(buf, halo, rows, w_ref, width):
    total = halo + rows
    full = buf[0:total, :]
    acc = None
    for phase in range(SUBLANES):
        taps = [j for j in range(width) if (halo - (width - 1) + j) % SUBLANES == phase]
        if not taps:
            continue
        shifted = full if phase == 0 else pltpu.roll(full, total - phase, 0)
        for j in taps:
            start = halo - (width - 1) + j - phase
            term = w_ref[j:j + 1, :] * shifted[start:start + rows, :]
            acc = term if acc is None else acc + term
    return acc


def _even_mixer_kernel(x_ref, mod_ref, gain_ref, win_ref, caw_ref, cab_ref, lng_ref, lnb_ref,
                       cbw_ref, cbb_ref, wa_ref, ba_ref, wx_ref, bx_ref, lam_ref, wout_ref,
                       o_ref, abuf, bbuf, hcar):
    rows = x_ref.shape[1]

    @pl.when(pl.program_id(1) == 0)
    def _():
        abuf[0:CONV_A_HALO, :] = jnp.zeros((CONV_A_HALO, MIX_W), F32)
        bbuf[0:CONV_S_HALO, :] = jnp.zeros((CONV_S_HALO, MIX_W), F32)
        hcar[...] = jnp.zeros(hcar.shape, F32)

    x = x_ref[0]
    h = _rms_mod(x, gain_ref[...], mod_ref[0, 1:2, :], mod_ref[0, 0:1, :])
    u = _dot(h.astype(BF16), win_ref[...])
    a_val = u[:, 0 * MIX_W:1 * MIX_W]
    a_gate = u[:, 1 * MIX_W:2 * MIX_W]
    b_x = u[:, 2 * MIX_W:3 * MIX_W]
    b_gate = u[:, 3 * MIX_W:4 * MIX_W]

    abuf[CONV_A_HALO:CONV_A_HALO + rows, :] = a_val * _sigmoid(a_gate)
    ya = _causal_conv(abuf, CONV_A_HALO, rows, caw_ref, CONF_WIDTH) + cab_ref[...]
    abuf[0:CONV_A_HALO, :] = abuf[rows:rows + CONV_A_HALO, :]
    mu = jnp.mean(ya, axis=-1, keepdims=True)
    dev = ya - mu
    var = jnp.mean(dev * dev, axis=-1, keepdims=True)
    ya = dev * lax.rsqrt(var + EPS) * lng_ref[...] + lnb_ref[...]
    ya = ya * _sigmoid(ya)

    bbuf[CONV_S_HALO:CONV_S_HALO + rows, :] = b_x
    xb = _causal_conv(bbuf, CONV_S_HALO, rows, cbw_ref, LRU_CONV_WIDTH) + cbb_ref[...]
    bbuf[0:CONV_S_HALO, :] = bbuf[rows:rows + CONV_S_HALO, :]
    xb16 = xb.astype(BF16)
    r_gate = _sigmoid(_dot(xb16, wa_ref[...]) + ba_ref[...])
    i_gate = _sigmoid(_dot(xb16, wx_ref[...]) + bx_ref[...])
    lam = lam_ref[...]
    log_sig = -(jnp.maximum(-lam, 0.0) + jnp.log1p(jnp.exp(-jnp.abs(lam))))
    log_a = LRU_C * r_gate * log_sig
    a = jnp.exp(log_a)
    mult = jnp.sqrt(-jnp.tanh(log_a) * (1.0 + a * a))
    bterm = mult * (i_gate * xb)

    row = lax.broadcasted_iota(jnp.int32, (rows, MIX_W), 0)
    d = 1
    while d < rows:
        keep = row >= d
        a_prev = jnp.where(keep, pltpu.roll(a, d, 0), 1.0)
        b_prev = jnp.where(keep, pltpu.roll(bterm, d, 0), 0.0)
        bterm = bterm + a * b_prev
        a = a * a_prev
        d *= 2
    hseq = bterm + a * hcar[0:1, :]
    hcar[...] = jnp.broadcast_to(hseq[rows - 1:rows, :], hcar.shape)

    gelu = 0.5 * b_gate * (1.0 + jnp.tanh(0.7978845608028654 * (b_gate + 0.044715 * (b_gate * b_gate * b_gate))))
    yb = hseq * gelu

    mix = _dot(ya.astype(BF16), wout_ref[0:MIX_W, :]) + _dot(yb.astype(BF16), wout_ref[MIX_W:2 * MIX_W, :])
    o_ref[0] = x + mod_ref[0, 2:3, :] * mix


def _even_mixer(x, mod, gain, w_in, caw, cab, lng, lnb, cbw, cbb, wa, ba, wx, bx, lam, w_out):
    b, s, d = x.shape
    rows = min(MIX_ROWS, s)
    row_spec = pl.BlockSpec((1, rows, d), lambda i, j: (i, j, 0))
    small = [gain, w_in, caw, cab, lng, lnb, cbw, cbb, wa, ba, wx, bx, lam, w_out]
    return pl.pallas_call(
        _even_mixer_kernel,
        grid=(b, s // rows),
        in_specs=[row_spec, pl.BlockSpec((1, 6, d), lambda i, j: (i, 0, 0))] + [_full(t.shape) for t in small],
        out_specs=row_spec,
        out_shape=jax.ShapeDtypeStruct(x.shape, F32),
        scratch_shapes=[pltpu.VMEM((CONV_A_HALO + rows, MIX_W), F32),
                        pltpu.VMEM((CONV_S_HALO + rows, MIX_W), F32),
                        pltpu.VMEM((SUBLANES, MIX_W), F32)],
        compiler_params=_params("arbitrary", "arbitrary"),
        name="even_mixer",
    )(x, mod, *small)


def _ffn_kernel(x_ref, mod_ref, gain_ref, wg_ref, wu_ref, wd_ref, o_ref, h_scr):
    @pl.when(pl.program_id(2) == 0)
    def _():
        x = x_ref[0]
        h_scr[...] = _rms_mod(x, gain_ref[...], mod_ref[0, 4:5, :], mod_ref[0, 3:4, :]).astype(BF16)
        o_ref[0] = x

    h = h_scr[...]
    g = _dot(h, wg_ref[...])
    act = (g * _sigmoid(g)) * _dot(h, wu_ref[...])
    o_ref[0] += mod_ref[0, 5:6, :] * _dot(act.astype(BF16), wd_ref[...])


def _dense_ffn(x, mod, gain, wg, wu, wd):
    b, s, d = x.shape
    f = wg.shape[1]
    rows = min(FFN_ROWS, s)
    cols = FFN_COLS if f % FFN_COLS == 0 else f
    row_spec = pl.BlockSpec((1, rows, d), lambda i, j, k: (i, j, 0))
    return pl.pallas_call(
        _ffn_kernel,
        grid=(b, s // rows, f // cols),
        in_specs=[row_spec,
                  pl.BlockSpec((1, 6, d), lambda i, j, k: (i, 0, 0)),
                  _full(gain.shape),
                  pl.BlockSpec((d, cols), lambda i, j, k: (0, k)),
                  pl.BlockSpec((d, cols), lambda i, j, k: (0, k)),
                  pl.BlockSpec((cols, d), lambda i, j, k: (k, 0))],
        out_specs=row_spec,
        out_shape=jax.ShapeDtypeStruct(x.shape, F32),
        scratch_shapes=[pltpu.VMEM((rows, d), BF16)],
        compiler_params=_params("arbitrary", "arbitrary", "arbitrary"),
        name="dense_ffn",
    )(x, mod, gain, wg, wu, wd)


def _odd_mixer_kernel(x_ref, mod_ref, gain_ref, win_ref, ccw_ref, qn_ref, kn_ref, seg_ref,
                      cos_ref, sa_ref, sb_ref, yc_ref, q_ref, k_ref, v_ref, km_ref, cbuf):
    rows = x_ref.shape[1]

    @pl.when(pl.program_id(1) == 0)
    def _():
        cbuf[0:CONV_S_HALO, :] = jnp.zeros((CONV_S_HALO, MIX_W), F32)

    x = x_ref[0]
    h = _rms_mod(x, gain_ref[...], mod_ref[0, 1:2, :], mod_ref[0, 0:1, :])
    u = _dot(h.astype(BF16), win_ref[...])
    c_h = u[:, 0 * MIX_W:1 * MIX_W]
    c_b = u[:, 1 * MIX_W:2 * MIX_W]
    c_c = u[:, 2 * MIX_W:3 * MIX_W]
    q = u[:, 3 * MIX_W:4 * MIX_W]
    k = u[:, 4 * MIX_W:5 * MIX_W]
    v = u[:, 5 * MIX_W:6 * MIX_W]

    cbuf[CONV_S_HALO:CONV_S_HALO + rows, :] = c_c * c_h
    conv = _causal_conv(cbuf, CONV_S_HALO, rows, ccw_ref, SHORT_CONV_WIDTH)
    cbuf[0:CONV_S_HALO, :] = cbuf[rows:rows + CONV_S_HALO, :]
    yc_ref[0] = (c_b * conv).astype(yc_ref.dtype)

    seg = seg_ref[...]
    cos = cos_ref[...]
    sin_a = sa_ref[...]
    sin_b = sb_ref[...]

    def head_norm_rope(t, gain):
        t2 = t * t
        hi = t2.astype(BF16)
        lo = (t2 - hi.astype(F32)).astype(BF16)
        ssq = _dot(hi, seg) + _dot(lo, seg)
        t = t * lax.rsqrt(ssq * (1.0 / GROUP_DIM) + EPS) * gain
        outs = []
        for g in range(MIX_W // LANES):
            tg = t[:, g * LANES:(g + 1) * LANES]
            outs.append(tg * cos + pltpu.roll(tg, LANES - GROUP_DIM // 2, 1) * sin_a
                        + pltpu.roll(tg, GROUP_DIM // 2, 1) * sin_b)
        return jnp.concatenate(outs, axis=1)

    qr = head_norm_rope(q, qn_ref[...])
    kr = head_norm_rope(k, kn_ref[...])
    q_ref[0] = (qr * (GROUP_DIM ** -0.5)).astype(q_ref.dtype)
    k_ref[0] = kr.astype(k_ref.dtype)
    v_ref[0] = v.astype(v_ref.dtype)
    for i in range(rows // MOBA_BLOCK):
        km_ref[0, i] = jnp.mean(kr[i * MOBA_BLOCK:(i + 1) * MOBA_BLOCK, :], axis=0, keepdims=True)


def _odd_mixer(x, mod, gain, w_in, ccw, qn, kn, seg, cos, sin_a, sin_b):
    b, s, d = x.shape
    rows = min(MIX_ROWS, s)
    nb = s // MOBA_BLOCK
    row_spec = pl.BlockSpec((1, rows, d), lambda i, j: (i, j, 0))
    mix_spec = pl.BlockSpec((1, rows, MIX_W), lambda i, j: (i, j, 0))
    tab_spec = pl.BlockSpec((rows, LANES), lambda i, j: (j, 0))
    small = [gain, w_in, ccw, qn, kn, seg]
    mix_shape = jax.ShapeDtypeStruct((b, s, MIX_W), BF16)
    return pl.pallas_call(
        _odd_mixer_kernel,
        grid=(b, s // rows),
        in_specs=[row_spec, pl.BlockSpec((1, 6, d), lambda i, j: (i, 0, 0))]
                 + [_full(t.shape) for t in small] + [tab_spec, tab_spec, tab_spec],
        out_specs=[mix_spec, mix_spec, mix_spec, mix_spec,
                   pl.BlockSpec((1, rows // MOBA_BLOCK, 1, MIX_W), lambda i, j: (i, j, 0, 0))],
        out_shape=[mix_shape, mix_shape, mix_shape, mix_shape,
                   jax.ShapeDtypeStruct((b, nb, 1, MIX_W), F32)],
        scratch_shapes=[pltpu.VMEM((CONV_S_HALO + rows, MIX_W), F32)],
        compiler_params=_params("arbitrary", "arbitrary"),
        name="odd_mixer",
    )(x, mod, *small, cos, sin_a, sin_b)


def _top_blocks(gate, own):
    nblk = gate.shape[0]
    blk = lax.broadcasted_iota(jnp.int32, gate.shape, 0)
    remaining = blk < own
    sel = jnp.zeros(gate.shape, F32)
    for _ in range(MOBA_TOPK):
        gm = jnp.where(remaining, gate, -jnp.inf)
        top = jnp.max(gm, axis=0, keepdims=True)
        cand = remaining & (gm == top)
        first = jnp.min(jnp.where(cand, blk, nblk), axis=0, keepdims=True)
        pick = cand & (blk == first)
        sel = jnp.where(pick, 1.0, sel)
        remaining = remaining & jnp.logical_not(pick)
    return sel


def _moba_kernel(q_ref, k_ref, v_ref, km_ref, *rest):
    n_cast = (len(rest) - 1) // 2
    o_ref = rest[n_cast]
    for src, dst in zip(rest[:n_cast], rest[n_cast + 1:]):
        dst[...] = src[...].astype(dst.dtype)
    own = pl.program_id(2)
    nblk = km_ref.shape[1]
    blk_rows = MOBA_BLOCK
    q = q_ref[0]
    lane = lax.broadcasted_iota(jnp.int32, q.shape, 1)
    zero = jnp.zeros_like(q)
    kmean = km_ref[0, :, 0, :].astype(BF16)
    row = lax.broadcasted_iota(jnp.int32, (blk_rows, blk_rows), 0)
    col = lax.broadcasted_iota(jnp.int32, (blk_rows, blk_rows), 1)
    causal_bias = jnp.where(col <= row, 0.0, -jnp.inf)
    pad = jnp.zeros((LANES - nblk, blk_rows), F32)

    q_heads, bias_heads = [], []
    for hh in range(LANES // GROUP_DIM):
        in_head = (lane >= hh * GROUP_DIM) & (lane < (hh + 1) * GROUP_DIM)
        qh = jnp.where(in_head, q, zero)
        sel_t = _top_blocks(_dot_nt(kmean, qh), own)
        sel = jnp.transpose(jnp.concatenate([sel_t, pad], axis=0))
        q_heads.append(qh)
        bias_heads.append(jnp.where(sel > 0.0, 0.0, -jnp.inf))

    group = MOBA_KEY_GROUP
    for g in range(-(-nblk // group)):
        nk = min((g + 1) * group, nblk)

        @pl.when(own // group == g)
        def _(g=g, nk=nk):
            keys = k_ref[0, 0:nk * blk_rows, :]
            vals = v_ref[0, 0:nk * blk_rows, :]
            outs = []
            for qh, bias in zip(q_heads, bias_heads):
                s = _dot_nt(qh, keys)
                pieces = []
                for jb in range(nk):
                    blk_bias = bias[:, jb:jb + 1]
                    if jb >= g * group:
                        blk_bias = jnp.where(own == jb, causal_bias, blk_bias)
                    pieces.append(s[:, jb * blk_rows:(jb + 1) * blk_rows] + blk_bias)
                s = jnp.concatenate(pieces, axis=1)
                m = jnp.max(s, axis=1, keepdims=True)
                p = jnp.exp(s - m)
                l = jnp.sum(p, axis=1, keepdims=True)
                outs.append(_dot(p.astype(BF16), vals) / l)
            o_ref[0] = jnp.where(lane < GROUP_DIM, outs[0], outs[1]).astype(o_ref.dtype)


def _moba(q, k, v, kmean, cast_weights):
    b, s, _ = q.shape
    nb = s // MOBA_BLOCK
    pairs = MIX_W // LANES
    steps = b * pairs * nb
    q_spec = pl.BlockSpec((1, MOBA_BLOCK, LANES), lambda i, p, j: (i, j, p))
    kv_spec = pl.BlockSpec((1, s, LANES), lambda i, p, j: (i, 0, p))
    bf16_rows = 2 * SUBLANES
    sliced = [w.reshape(steps, -1, w.shape[-1]) for w in cast_weights]
    assert all(w.shape[1] % bf16_rows == 0 and w.shape[2] % LANES == 0 for w in sliced)
    w_specs = [pl.BlockSpec((1,) + w.shape[1:], lambda i, p, j: ((i * pairs + p) * nb + j, 0, 0)) for w in sliced]
    outs = pl.pallas_call(
        _moba_kernel,
        grid=(b, pairs, nb),
        in_specs=[q_spec, kv_spec, kv_spec,
                  pl.BlockSpec((1, nb, 1, LANES), lambda i, p, j: (i, 0, 0, p))] + w_specs,
        out_specs=[q_spec] + w_specs,
        out_shape=[jax.ShapeDtypeStruct(q.shape, BF16)] + [jax.ShapeDtypeStruct(w.shape, BF16) for w in sliced],
        compiler_params=_params("arbitrary", "arbitrary", "arbitrary"),
        name="moba_attention",
    )(q, k, v, kmean, *sliced)
    return outs[0], [o.reshape(w.shape) for o, w in zip(outs[1:], cast_weights)]


def _odd_out_kernel(yc_ref, yd_ref, x_ref, mod_ref, wout_ref, gain_ref, rwh_ref, rwl_ref, rb_ref,
                    x2_ref, h2_ref, wt_ref, mk_ref, cnt_ref):
    x = x_ref[0]
    mix = _dot(yc_ref[0], wout_ref[0:MIX_W, :]) + _dot(yd_ref[0], wout_ref[MIX_W:2 * MIX_W, :])
    x2 = x + mod_ref[0, 2:3, :] * mix
    x2_ref[0] = x2
    h = _rms_mod(x2, gain_ref[...], mod_ref[0, 4:5, :], mod_ref[0, 3:4, :])
    h_hi = h.astype(BF16)
    h2_ref[0] = h_hi.astype(h2_ref.dtype)

    h_lo = (h - h_hi.astype(F32)).astype(BF16)
    rwh = rwh_ref[...]
    logits = _dot(h_hi, rwh) + _dot(h_lo, rwh) + _dot(h_hi, rwl_ref[...]) + rb_ref[...]
    lane = lax.broadcasted_iota(jnp.int32, logits.shape, 1)
    valid = lane < N_EXPERTS
    lg = jnp.where(valid, logits, -jnp.inf)
    top1 = jnp.max(lg, axis=1, keepdims=True)
    idx1 = jnp.min(jnp.where(lg == top1, lane, LANES), axis=1, keepdims=True)
    pick1 = lane == idx1
    lg2 = jnp.where(pick1, -jnp.inf, lg)
    top2 = jnp.max(lg2, axis=1, keepdims=True)
    idx2 = jnp.min(jnp.where((lg2 == top2) & valid & jnp.logical_not(pick1), lane, LANES), axis=1, keepdims=True)
    pick2 = lane == idx2
    e2 = jnp.exp(top2 - top1)
    den = 1.0 + e2
    wt_ref[0] = jnp.where(pick1, 1.0 / den, 0.0) + jnp.where(pick2, e2 / den, 0.0)
    mask = jnp.where(pick1 | pick2, 1.0, 0.0)
    mk_ref[0] = mask.astype(mk_ref.dtype)
    cnt_ref[0, 0] = jnp.sum(mask, axis=0, keepdims=True)


def _odd_out(yc, yd, x, mod, w_out, gain, rw_hi, rw_lo, rb):
    b, s, d = x.shape
    rows = min(MIX_ROWS, s)
    row_spec = pl.BlockSpec((1, rows, d), lambda i, j: (i, j, 0))
    mix_spec = pl.BlockSpec((1, rows, MIX_W), lambda i, j: (i, j, 0))
    lane_spec = pl.BlockSpec((1, rows, LANES), lambda i, j: (i, j, 0))
    small = [w_out, gain, rw_hi, rw_lo, rb]
    return pl.pallas_call(
        _odd_out_kernel,
        grid=(b, s // rows),
        in_specs=[mix_spec, mix_spec, row_spec, pl.BlockSpec((1, 6, d), lambda i, j: (i, 0, 0))]
                 + [_full(t.shape) for t in small],
        out_specs=[row_spec, row_spec, lane_spec, lane_spec,
                   pl.BlockSpec((1, 1, 1, LANES), lambda i, j: (i, j, 0, 0))],
        out_shape=[jax.ShapeDtypeStruct(x.shape, F32),
                   jax.ShapeDtypeStruct(x.shape, BF16),
                   jax.ShapeDtypeStruct((b, s, LANES), F32),
                   jax.ShapeDtypeStruct((b, s, LANES), BF16),
                   jax.ShapeDtypeStruct((b, s // rows, 1, LANES), F32)],
        compiler_params=_params("arbitrary", "arbitrary"),
        name="odd_out_router",
    )(yc, yd, x, mod, *small)


def _moe_kernel(cnt_ref, cume_ref, h2_ref, wt_ref, mk_ref, x2_ref, mod_ref, wg_ref, wu_ref, wd_ref, o_ref,
                posc_scr, posr_scr, xs_scr, acc_scr, posb_scr, wb_scr):
    tile = pl.program_id(0)
    e = pl.program_id(1)
    f = pl.program_id(2)
    rows = h2_ref.shape[0]
    ch = MOE_CHUNK
    big, med, small = MOE_BIG_ROWS, MOE_MED_ROWS, MOE_SMALL_ROWS
    count = cnt_ref[tile * N_EXPERTS + e]
    units = lax.shift_right_logical(count + (small - 1), small.bit_length() - 1)
    n_big = lax.shift_right_logical(units, 2)
    n_med = lax.shift_right_logical(units & 3, 1)
    n_small = units & 1
    med_base = n_big * big
    small_base = med_base + n_med * med

    def for_blocks(body):
        lax.fori_loop(0, n_big, lambda kk, c: body(pl.multiple_of(kk * big, big), big) or c, 0)
        lax.fori_loop(0, n_med, lambda kk, c: body(pl.multiple_of(med_base, med), med) or c, 0)
        lax.fori_loop(0, n_small, lambda kk, c: body(pl.multiple_of(small_base, small), small) or c, 0)

    @pl.when((e == 0) & (f == 0))
    def _positions():
        o_ref[...] = jnp.zeros(o_ref.shape, F32)
        r_i = lax.broadcasted_iota(jnp.int32, (ch, ch), 0)
        c_i = lax.broadcasted_iota(jnp.int32, (ch, ch), 1)
        lower = jnp.where(c_i < r_i, 1.0, 0.0).astype(BF16)
        upper = jnp.where(r_i < c_i, 1.0, 0.0).astype(BF16)
        eye = jnp.where(lax.broadcasted_iota(jnp.int32, (N_EXPERTS, LANES), 0)
                        == lax.broadcasted_iota(jnp.int32, (N_EXPERTS, LANES), 1), 1.0, 0.0).astype(BF16)
        carry_c = jnp.zeros((1, LANES), F32)
        carry_r = jnp.zeros((N_EXPERTS, 1), F32)
        for blk in range(rows // ch):
            sl = slice(blk * ch, (blk + 1) * ch)
            mb = mk_ref[sl, :]
            mbf = mb.astype(F32)
            posc_scr[sl, :] = jnp.where(mbf > 0.0, _dot(lower, mb) + carry_c, -1.0)
            carry_c = carry_c + jnp.sum(mbf, axis=0, keepdims=True)
            mbt = _dot_nt(eye, mb)
            posr_scr[:, sl] = jnp.where(mbt > 0.0, _dot(mbt.astype(BF16), upper) + carry_r, -1.0)
            carry_r = carry_r + jnp.sum(mbt, axis=1, keepdims=True)

    sub = MOE_SCATTER_ROWS
    nsub = rows // sub
    win = MOE_WINDOW_ROWS
    cum_base = (tile * N_EXPERTS + e) * (nsub + 1)
    width = acc_scr.shape[1]

    def windows(tb):
        lo = cume_ref[cum_base + tb]
        hi = cume_ref[cum_base + tb + 1]
        start = lax.shift_left(lax.shift_right_logical(lo, small.bit_length() - 1), small.bit_length() - 1)
        n_win = lax.shift_right_logical(hi - start + (win - 1), win.bit_length() - 1)
        return start, jnp.where(hi > lo, n_win, 0)

    def unit_rows(u):
        return pl.ds(pl.multiple_of(u * small, small), small)

    @pl.when(f == 0)
    def _gather():
        def zero(u, c):
            acc_scr[unit_rows(u), :] = jnp.zeros((small, width), F32)
            return c

        lax.fori_loop(0, units + win // small, zero, 0)
        r_i = lax.broadcasted_iota(jnp.int32, (win, sub), 0).astype(F32)
        for tb in range(nsub):
            start, n_win = windows(tb)
            posr = posr_scr[pl.ds(e, 1), tb * sub:(tb + 1) * sub]
            h2_tb = h2_ref[tb * sub:(tb + 1) * sub, :]

            def gather_window(k, c, start=start, posr=posr, h2_tb=h2_tb):
                off = pl.multiple_of(start + k * win, small)
                onehot = jnp.where(posr - off.astype(F32) == r_i, 1.0, 0.0).astype(BF16)
                acc_scr[pl.ds(off, win), :] += _dot(onehot, h2_tb)
                return c

            lax.fori_loop(0, n_win, gather_window, 0)

        def to_bf16(u, c):
            xs_scr[unit_rows(u), :] = acc_scr[unit_rows(u), :].astype(xs_scr.dtype)
            acc_scr[unit_rows(u), :] = jnp.zeros((small, width), F32)
            return c

        lax.fori_loop(0, units, to_bf16, 0)

    def ffn_body(off, m):
        xk = xs_scr[pl.ds(off, m), :]
        g = _dot(xk, wg_ref[0])
        act = (g * _sigmoid(g)) * _dot(xk, wu_ref[0])
        acc_scr[pl.ds(off, m), :] += _dot(act.astype(BF16), wd_ref[0])

    for_blocks(ffn_body)

    @pl.when(f == pl.num_programs(2) - 1)
    def _scatter():
        lane = lax.broadcasted_iota(jnp.int32, (sub, LANES), 1)
        lane_f = lane.astype(F32)
        reps = width // LANES
        for tb in range(nsub):
            sl = slice(tb * sub, (tb + 1) * sub)
            pos_e = jnp.sum(jnp.where(lane == e, posc_scr[sl, :], 0.0), axis=1, keepdims=True)
            w_e = jnp.sum(jnp.where(lane == e, wt_ref[sl, :], 0.0), axis=1, keepdims=True)
            posb_scr[...] = jnp.broadcast_to(pos_e, (sub, LANES)) - lane_f
            wb_scr[...] = jnp.broadcast_to(w_e, (sub, LANES))
            start, n_win = windows(tb)

            def scatter_window(k, c, start=start, sl=sl):
                off = pl.multiple_of(start + k * win, small)
                y = acc_scr[pl.ds(off, win), :].astype(BF16)
                rel = posb_scr[...] - off.astype(F32)
                onehot = jnp.concatenate(
                    [jnp.where(rel == float(g * LANES), 1.0, 0.0) for g in range(win // LANES)], axis=1).astype(BF16)
                o_ref[sl, :] += jnp.concatenate([wb_scr[...]] * reps, axis=1) * _dot(onehot, y)
                return c

            lax.fori_loop(0, n_win, scatter_window, 0)

        @pl.when(e == N_EXPERTS - 1)
        def _residual():
            o_ref[...] = x2_ref[...] + mod_ref[0, 5:6, :] * o_ref[...]


def _moe(h2, wts, mask, x2, mod, counts, cum_counts, wg, wu, wd, seq):
    t, d = h2.shape
    fdim = wg.shape[2]
    cols = MOE_COLS if fdim % MOE_COLS == 0 else fdim
    n_f = fdim // cols
    rows = min(MOE_ROWS, seq)
    once = pl.Buffered(1)
    tile_spec = lambda width: pl.BlockSpec((rows, width), lambda i, e, f, n, c: (i, 0), pipeline_mode=once)
    grid_spec = pltpu.PrefetchScalarGridSpec(
        num_scalar_prefetch=2,
        grid=(t // rows, N_EXPERTS, n_f),
        in_specs=[tile_spec(d), tile_spec(LANES), tile_spec(LANES), tile_spec(d),
                  pl.BlockSpec((1, 6, d), lambda i, e, f, n, c: (i * rows // seq, 0, 0)),
                  pl.BlockSpec((1, d, cols), lambda i, e, f, n, c: (e, 0, f)),
                  pl.BlockSpec((1, d, cols), lambda i, e, f, n, c: (e, 0, f)),
                  pl.BlockSpec((1, cols, d), lambda i, e, f, n, c: (e, f, 0))],
        out_specs=pl.BlockSpec((rows, d), lambda i, e, f, n, c: (i, 0), pipeline_mode=once),
        scratch_shapes=[pltpu.VMEM((rows, LANES), F32),
                        pltpu.VMEM((N_EXPERTS, rows), F32),
                        pltpu.VMEM((rows, d), BF16),
                        pltpu.VMEM((rows + MOE_WINDOW_ROWS, d), F32),
                        pltpu.VMEM((MOE_SCATTER_ROWS, LANES), F32),
                        pltpu.VMEM((MOE_SCATTER_ROWS, LANES), F32)],
    )
    return pl.pallas_call(
        _moe_kernel,
        grid_spec=grid_spec,
        out_shape=jax.ShapeDtypeStruct((t, d), F32),
        compiler_params=_params("arbitrary", "arbitrary", "arbitrary"),
        name="moe_experts",
    )(counts, cum_counts, h2, wts, mask, x2, mod, wg, wu, wd)


def _block_diag(w):
    h, i, j = w.shape
    eye = jnp.eye(h, dtype=w.dtype)
    return jnp.einsum('hij,hk->hikj', w, eye).reshape(h * i, h * j)


def _rope_tables(seq):
    half = GROUP_DIM // 2
    inv = ROPE_THETA ** (-jnp.arange(half, dtype=F32) / half)
    ang = jnp.arange(seq).astype(F32)[:, None] * inv[None, :]
    cos, sin, zero = jnp.cos(ang), jnp.sin(ang), jnp.zeros_like(ang)
    reps = LANES // GROUP_DIM
    return (jnp.tile(jnp.concatenate([cos, cos], axis=1), (1, reps)),
            jnp.tile(jnp.concatenate([-sin, zero], axis=1), (1, reps)),
            jnp.tile(jnp.concatenate([zero, sin], axis=1), (1, reps)))


def _row(v):
    return v.reshape(1, -1).astype(F32)


def kernel(x, c, e_ada_w, e_ada_b, e_norm_mix, e_norm_ffn, e_w_in, e_conv_a_w, e_conv_a_b,
           e_ln_a_g, e_ln_a_b, e_conv_b_w, e_conv_b_b, e_lru_wa, e_lru_ba, e_lru_wx, e_lru_bx,
           e_lru_lambda, e_w_out, e_ffn_wg, e_ffn_wu, e_ffn_wd,
           o_ada_w, o_ada_b, o_norm_mix, o_norm_ffn, o_w_in, o_conv_c_w, o_q_norm, o_k_norm,
           o_w_out, o_router_w, o_router_b, o_moe_wg, o_moe_wu, o_moe_wd):
    b, s, d = x.shape
    assert s % MOBA_BLOCK == 0 and d % LANES == 0
    c_pad = jnp.zeros((SUBLANES, d), F32).at[:b].set(c.astype(F32))
    seg = jnp.kron(jnp.eye(N_GROUPS, dtype=F32), jnp.ones((GROUP_DIM, GROUP_DIM), F32)).astype(BF16)
    cos, sin_a, sin_b = _rope_tables(s)

    def modulation(w, bias):
        return _modulation(c_pad, w, _row(bias))[:b].reshape(b, 6, d)

    x = x.astype(F32)
    for layer in range(DEPTH):
        j = layer // 2
        if layer % 2 == 0:
            mod = modulation(e_ada_w[j], e_ada_b[j])
            x = _even_mixer(
                x, mod, _row(e_norm_mix[j]), e_w_in[j].astype(BF16),
                e_conv_a_w[j].astype(F32), _row(e_conv_a_b[j]), _row(e_ln_a_g[j]), _row(e_ln_a_b[j]),
                e_conv_b_w[j].astype(F32), _row(e_conv_b_b[j]),
                _block_diag(e_lru_wa[j]).astype(BF16), _row(e_lru_ba[j]),
                _block_diag(e_lru_wx[j]).astype(BF16), _row(e_lru_bx[j]),
                _row(e_lru_lambda[j]), e_w_out[j].astype(BF16))
            x = _dense_ffn(x, mod, _row(e_norm_ffn[j]), e_ffn_wg[j].astype(BF16),
                           e_ffn_wu[j].astype(BF16), e_ffn_wd[j].astype(BF16))
        else:
            mod = modulation(o_ada_w[j], o_ada_b[j])
            yc, q, k, v, kmean = _odd_mixer(
                x, mod, _row(o_norm_mix[j]), o_w_in[j].astype(BF16), o_conv_c_w[j].astype(F32),
                _row(jnp.tile(o_q_norm[j], N_GROUPS)), _row(jnp.tile(o_k_norm[j], N_GROUPS)),
                seg, cos, sin_a, sin_b)
            yd, (moe_wg, moe_wu, moe_wd) = _moba(
                q, k, v, kmean, [o_moe_wg[j].astype(F32), o_moe_wu[j].astype(F32), o_moe_wd[j].astype(F32)])
            rw = jnp.zeros((d, LANES), F32).at[:, :N_EXPERTS].set(o_router_w[j].astype(F32))
            rw_hi = rw.astype(BF16)
            rw_lo = (rw - rw_hi.astype(F32)).astype(BF16)
            rb = jnp.zeros((1, LANES), F32).at[0, :N_EXPERTS].set(o_router_b[j].astype(F32))
            x2, h2, wts, mask, cnt = _odd_out(yc, yd, x, mod, o_w_out[j].astype(BF16),
                                              _row(o_norm_ffn[j]), rw_hi, rw_lo, rb)
            rows = min(MOE_ROWS, s)
            assert MIX_ROWS == MOE_SCATTER_ROWS or s < MIX_ROWS
            sub_cnt = cnt.reshape(b * s // rows, -1, LANES)[:, :, :N_EXPERTS].astype(jnp.int32)
            cum = jnp.cumsum(sub_cnt, axis=1)
            cum = jnp.concatenate([jnp.zeros_like(cum[:, :1]), cum], axis=1)
            counts = cum[:, -1, :].reshape(-1)
            cum_counts = jnp.transpose(cum, (0, 2, 1)).reshape(-1)
            out = _moe(h2.reshape(b * s, d), wts.reshape(b * s, LANES), mask.reshape(b * s, LANES),
                       x2.reshape(b * s, d), mod, counts, cum_counts,
                       moe_wg, moe_wu, moe_wd, s)
            x = out.reshape(b, s, d)
    return x
```

```python
import functools

import jax
import jax.numpy as jnp
from jax import lax
from jax.experimental import pallas as pl
from jax.experimental.pallas import tpu as pltpu

F32 = jnp.float32
BF16 = jnp.bfloat16

N_GROUPS = 8
GROUP_DIM = 64
MIX_W = N_GROUPS * GROUP_DIM
CONF_WIDTH = 31
LRU_CONV_WIDTH = 4
LRU_C = 8.0
SHORT_CONV_WIDTH = 3
MOBA_BLOCK = 256
MOBA_TOPK = 3
ROPE_THETA = 10000.0
N_EXPERTS = 8
EPS = 1e-6
DEPTH = 2

LANES = 128
SUBLANES = 8
VMEM_LIMIT_BYTES = 62 * 1024 * 1024

MIX_ROWS = 512
FFN_ROWS = 512
FFN_COLS = 2816
MOE_ROWS = 2048
MOE_COLS = 1792
MOE_CHUNK = 256
MOE_BIG_ROWS = 512
MOE_MED_ROWS = 256
MOE_SMALL_ROWS = 128
MOE_SCATTER_ROWS = 512
MOE_WINDOW_ROWS = 256
MOBA_KEY_GROUP = 2
CONV_A_HALO = 32
CONV_S_HALO = 8


def _dot(a, b):
    return jnp.dot(a, b, preferred_element_type=F32)


def _dot_nt(a, b):
    return lax.dot_general(a, b, (((1,), (1,)), ((), ())), preferred_element_type=F32)


def _sigmoid(x):
    return 1.0 / (1.0 + jnp.exp(-x))


def _rms_mod(x, gain, scale, shift):
    ms = jnp.mean(x * x, axis=-1, keepdims=True)
    return (x * lax.rsqrt(ms + EPS)) * gain * (1.0 + scale) + shift


def _params(*sem):
    return pltpu.CompilerParams(dimension_semantics=sem, vmem_limit_bytes=VMEM_LIMIT_BYTES)


def _full(shape):
    n = len(shape)
    return pl.BlockSpec(shape, lambda *_: (0,) * n)


def _mod_kernel(c_ref, w_ref, b_ref, o_ref):
    c = c_ref[...]
    c_act = c * _sigmoid(c)
    o_ref[...] = _dot(c_act.astype(BF16), w_ref[...].astype(BF16)) + b_ref[...]


def _modulation(c_pad, w, b):
    d, n = w.shape
    tn = n // 4
    return pl.pallas_call(
        _mod_kernel,
        grid=(4,),
        in_specs=[_full(c_pad.shape),
                  pl.BlockSpec((d, tn), lambda j: (0, j)),
                  pl.BlockSpec((1, tn), lambda j: (0, j))],
        out_specs=pl.BlockSpec((c_pad.shape[0], tn), lambda j: (0, j)),
        out_shape=jax.ShapeDtypeStruct((c_pad.shape[0], n), F32),
        compiler_params=_params("arbitrary"),
        name="adaln_mod",
    )(c_pad, w, b)


def _causal_conv(buf, halo, rows, w_ref, width):
    total = halo + rows
    full = buf[0:total, :]
    acc = None
    for phase in range(SUBLANES):
        taps = [j for j in range(width) if (halo - (width - 1) + j) % SUBLANES == phase]
        if not taps:
            continue
        shifted = full if phase == 0 else pltpu.roll(full, total - phase, 0)
        for j in taps:
            start = halo - (width - 1) + j - phase
            term = w_ref[j:j + 1, :] * shifted[start:start + rows, :]
            acc = term if acc is None else acc + term
    return acc


def _even_mixer_kernel(x_ref, mod_ref, gain_ref, win_ref, caw_ref, cab_ref, lng_ref, lnb_ref,
                       cbw_ref, cbb_ref, wa_ref, ba_ref, wx_ref, bx_ref, lam_ref, wout_ref,
                       o_ref, abuf, bbuf, hcar):
    rows = x_ref.shape[1]

    @pl.when(pl.program_id(1) == 0)
    def _():
        abuf[0:CONV_A_HALO, :] = jnp.zeros((CONV_A_HALO, MIX_W), F32)
        bbuf[0:CONV_S_HALO, :] = jnp.zeros((CONV_S_HALO, MIX_W), F32)
        hcar[...] = jnp.zeros(hcar.shape, F32)

    x = x_ref[0]
    h = _rms_mod(x, gain_ref[...], mod_ref[0, 1:2, :], mod_ref[0, 0:1, :])
    u = _dot(h.astype(BF16), win_ref[...])
    a_val = u[:, 0 * MIX_W:1 * MIX_W]
    a_gate = u[:, 1 * MIX_W:2 * MIX_W]
    b_x = u[:, 2 * MIX_W:3 * MIX_W]
    b_gate = u[:, 3 * MIX_W:4 * MIX_W]

    abuf[CONV_A_HALO:CONV_A_HALO + rows, :] = a_val * _sigmoid(a_gate)
    ya = _causal_conv(abuf, CONV_A_HALO, rows, caw_ref, CONF_WIDTH) + cab_ref[...]
    abuf[0:CONV_A_HALO, :] = abuf[rows:rows + CONV_A_HALO, :]
    mu = jnp.mean(ya, axis=-1, keepdims=True)
    dev = ya - mu
    var = jnp.mean(dev * dev, axis=-1, keepdims=True)
    ya = dev * lax.rsqrt(var + EPS) * lng_ref[...] + lnb_ref[...]
    ya = ya * _sigmoid(ya)

    bbuf[CONV_S_HALO:CONV_S_HALO + rows, :] = b_x
    xb = _causal_conv(bbuf, CONV_S_HALO, rows, cbw_ref, LRU_CONV_WIDTH) + cbb_ref[...]
    bbuf[0:CONV_S_HALO, :] = bbuf[rows:rows + CONV_S_HALO, :]
    xb16 = xb.astype(BF16)
    r_gate = _sigmoid(_dot(xb16, wa_ref[...]) + ba_ref[...])
    i_gate = _sigmoid(_dot(xb16, wx_ref[...]) + bx_ref[...])
    lam = lam_ref[...]
    log_sig = -(jnp.maximum(-lam, 0.0) + jnp.log1p(jnp.exp(-jnp.abs(lam))))
    log_a = LRU_C * r_gate * log_sig
    a = jnp.exp(log_a)
    mult = jnp.sqrt(-jnp.tanh(log_a) * (1.0 + a * a))
    bterm = mult * (i_gate * xb)

    row = lax.broadcasted_iota(jnp.int32, (rows, MIX_W), 0)
    d = 1
    while d < rows:
        keep = row >= d
        a_prev = jnp.where(keep, pltpu.roll(a, d, 0), 1.0)
        b_prev = jnp.where(keep, pltpu.roll(bterm, d, 0), 0.0)
        bterm = bterm + a * b_prev
        a = a * a_prev
        d *= 2
    hseq = bterm + a * hcar[0:1, :]
    hcar[...] = jnp.broadcast_to(hseq[rows - 1:rows, :], hcar.shape)

    gelu = 0.5 * b_gate * (1.0 + jnp.tanh(0.7978845608028654 * (b_gate + 0.044715 * (b_gate * b_gate * b_gate))))
    yb = hseq * gelu

    mix = _dot(ya.astype(BF16), wout_ref[0:MIX_W, :]) + _dot(yb.astype(BF16), wout_ref[MIX_W:2 * MIX_W, :])
    o_ref[0] = x + mod_ref[0, 2:3, :] * mix


def _even_mixer(x, mod, gain, w_in, caw, cab, lng, lnb, cbw, cbb, wa, ba, wx, bx, lam, w_out):
    b, s, d = x.shape
    rows = min(MIX_ROWS, s)
    row_spec = pl.BlockSpec((1, rows, d), lambda i, j: (i, j, 0))
    small = [gain, w_in, caw, cab, lng, lnb, cbw, cbb, wa, ba, wx, bx, lam, w_out]
    return pl.pallas_call(
        _even_mixer_kernel,
        grid=(b, s // rows),
        in_specs=[row_spec, pl.BlockSpec((1, 6, d), lambda i, j: (i, 0, 0))] + [_full(t.shape) for t in small],
        out_specs=row_spec,
        out_shape=jax.ShapeDtypeStruct(x.shape, F32),
        scratch_shapes=[pltpu.VMEM((CONV_A_HALO + rows, MIX_W), F32),
                        pltpu.VMEM((CONV_S_HALO + rows, MIX_W), F32),
                        pltpu.VMEM((SUBLANES, MIX_W), F32)],
        compiler_params=_params("arbitrary", "arbitrary"),
        name="even_mixer",
    )(x, mod, *small)


def _ffn_kernel(x_ref, mod_ref, gain_ref, wg_ref, wu_ref, wd_ref, o_ref, h_scr):
    @pl.when(pl.program_id(2) == 0)
    def _():
        x = x_ref[0]
        h_scr[...] = _rms_mod(x, gain_ref[...], mod_ref[0, 4:5, :], mod_ref[0, 3:4, :]).astype(BF16)
        o_ref[0] = x

    h = h_scr[...]
    g = _dot(h, wg_ref[...])
    act = (g * _sigmoid(g)) * _dot(h, wu_ref[...])
    o_ref[0] += mod_ref[0, 5:6, :] * _dot(act.astype(BF16), wd_ref[...])


def _dense_ffn(x, mod, gain, wg, wu, wd):
    b, s, d = x.shape
    f = wg.shape[1]
    rows = min(FFN_ROWS, s)
    cols = FFN_COLS if f % FFN_COLS == 0 else f
    row_spec = pl.BlockSpec((1, rows, d), lambda i, j, k: (i, j, 0))
    return pl.pallas_call(
        _ffn_kernel,
        grid=(b, s // rows, f // cols),
        in_specs=[row_spec,
                  pl.BlockSpec((1, 6, d), lambda i, j, k: (i, 0, 0)),
                  _full(gain.shape),
                  pl.BlockSpec((d, cols), lambda i, j, k: (0, k)),
                  pl.BlockSpec((d, cols), lambda i, j, k: (0, k)),
                  pl.BlockSpec((cols, d), lambda i, j, k: (k, 0))],
        out_specs=row_spec,
        out_shape=jax.ShapeDtypeStruct(x.shape, F32),
        scratch_shapes=[pltpu.VMEM((rows, d), BF16)],
        compiler_params=_params("arbitrary", "arbitrary", "arbitrary"),
        name="dense_ffn",
    )(x, mod, gain, wg, wu, wd)


def _odd_mixer_kernel(x_ref, mod_ref, gain_ref, win_ref, ccw_ref, qn_ref, kn_ref, seg_ref,
                      cos_ref, sa_ref, sb_ref, yc_ref, q_ref, k_ref, v_ref, km_ref, cbuf):
    rows = x_ref.shape[1]

    @pl.when(pl.program_id(1) == 0)
    def _():
        cbuf[0:CONV_S_HALO, :] = jnp.zeros((CONV_S_HALO, MIX_W), F32)

    x = x_ref[0]
    h = _rms_mod(x, gain_ref[...], mod_ref[0, 1:2, :], mod_ref[0, 0:1, :])
    u = _dot(h.astype(BF16), win_ref[...])
    c_h = u[:, 0 * MIX_W:1 * MIX_W]
    c_b = u[:, 1 * MIX_W:2 * MIX_W]
    c_c = u[:, 2 * MIX_W:3 * MIX_W]
    q = u[:, 3 * MIX_W:4 * MIX_W]
    k = u[:, 4 * MIX_W:5 * MIX_W]
    v = u[:, 5 * MIX_W:6 * MIX_W]

    cbuf[CONV_S_HALO:CONV_S_HALO + rows, :] = c_c * c_h
    conv = _causal_conv(cbuf, CONV_S_HALO, rows, ccw_ref, SHORT_CONV_WIDTH)
    cbuf[0:CONV_S_HALO, :] = cbuf[rows:rows + CONV_S_HALO, :]
    yc_ref[0] = (c_b * conv).astype(yc_ref.dtype)

    seg = seg_ref[...]
    cos = cos_ref[...]
    sin_a = sa_ref[...]
    sin_b = sb_ref[...]

    def head_norm_rope(t, gain):
        t2 = t * t
        hi = t2.astype(BF16)
        lo = (t2 - hi.astype(F32)).astype(BF16)
        ssq = _dot(hi, seg) + _dot(lo, seg)
        t = t * lax.rsqrt(ssq * (1.0 / GROUP_DIM) + EPS) * gain
        outs = []
        for g in range(MIX_W // LANES):
            tg = t[:, g * LANES:(g + 1) * LANES]
            outs.append(tg * cos + pltpu.roll(tg, LANES - GROUP_DIM // 2, 1) * sin_a
                        + pltpu.roll(tg, GROUP_DIM // 2, 1) * sin_b)
        return jnp.concatenate(outs, axis=1)

    qr = head_norm_rope(q, qn_ref[...])
    kr = head_norm_rope(k, kn_ref[...])
    q_ref[0] = (qr * (GROUP_DIM ** -0.5)).astype(q_ref.dtype)
    k_ref[0] = kr.astype(k_ref.dtype)
    v_ref[0] = v.astype(v_ref.dtype)
    for i in range(rows // MOBA_BLOCK):
        km_ref[0, i] = jnp.mean(kr[i * MOBA_BLOCK:(i + 1) * MOBA_BLOCK, :], axis=0, keepdims=True)


def _odd_mixer(x, mod, gain, w_in, ccw, qn, kn, seg, cos, sin_a, sin_b):
    b, s, d = x.shape
    rows = min(MIX_ROWS, s)
    nb = s // MOBA_BLOCK
    row_spec = pl.BlockSpec((1, rows, d), lambda i, j: (i, j, 0))
    mix_spec = pl.BlockSpec((1, rows, MIX_W), lambda i, j: (i, j, 0))
    tab_spec = pl.BlockSpec((rows, LANES), lambda i, j: (j, 0))
    small = [gain, w_in, ccw, qn, kn, seg]
    mix_shape = jax.ShapeDtypeStruct((b, s, MIX_W), BF16)
    return pl.pallas_call(
        _odd_mixer_kernel,
        grid=(b, s // rows),
        in_specs=[row_spec, pl.BlockSpec((1, 6, d), lambda i, j: (i, 0, 0))]
                 + [_full(t.shape) for t in small] + [tab_spec, tab_spec, tab_spec],
        out_specs=[mix_spec, mix_spec, mix_spec, mix_spec,
                   pl.BlockSpec((1, rows // MOBA_BLOCK, 1, MIX_W), lambda i, j: (i, j, 0, 0))],
        out_shape=[mix_shape, mix_shape, mix_shape, mix_shape,
                   jax.ShapeDtypeStruct((b, nb, 1, MIX_W), F32)],
        scratch_shapes=[pltpu.VMEM((CONV_S_HALO + rows, MIX_W), F32)],
        compiler_params=_params("arbitrary", "arbitrary"),
        name="odd_mixer",
    )(x, mod, *small, cos, sin_a, sin_b)


def _top_blocks(gate, own):
    nblk = gate.shape[0]
    blk = lax.broadcasted_iota(jnp.int32, gate.shape, 0)
    remaining = blk < own
    sel = jnp.zeros(gate.shape, F32)
    for _ in range(MOBA_TOPK):
        gm = jnp.where(remaining, gate, -jnp.inf)
        top = jnp.max(gm, axis=0, keepdims=True)
        cand = remaining & (gm == top)
        first = jnp.min(jnp.where(cand, blk, nblk), axis=0, keepdims=True)
        pick = cand & (blk == first)
        sel = jnp.where(pick, 1.0, sel)
        remaining = remaining & jnp.logical_not(pick)
    return sel


def _moba_kernel(q_ref, k_ref, v_ref, km_ref, *rest):
    n_cast = (len(rest) - 1) // 2
    o_ref = rest[n_cast]
    for src, dst in zip(rest[:n_cast], rest[n_cast + 1:]):
        dst[...] = src[...].astype(dst.dtype)
    group = MOBA_KEY_GROUP
    step = pl.program_id(2)
    nblk = km_ref.shape[1]
    blk_rows = MOBA_BLOCK
    q = q_ref[0]
    q_rows = q.shape[0]
    lane = lax.broadcasted_iota(jnp.int32, q.shape, 1)
    zero = jnp.zeros_like(q)
    kmean = km_ref[0, :, 0, :].astype(BF16)
    row = lax.broadcasted_iota(jnp.int32, (blk_rows, blk_rows), 0)
    col = lax.broadcasted_iota(jnp.int32, (blk_rows, blk_rows), 1)
    causal_bias = jnp.where(col <= row, 0.0, -jnp.inf)
    future_bias = jnp.full((blk_rows, blk_rows), -jnp.inf, F32)
    pad = jnp.zeros((LANES - nblk, q_rows), F32)
    q_pos = lax.broadcasted_iota(jnp.int32, (1, q_rows), 1)
    own = step * group + lax.shift_right_logical(q_pos, blk_rows.bit_length() - 1)

    q_heads, bias_heads = [], []
    for hh in range(LANES // GROUP_DIM):
        in_head = (lane >= hh * GROUP_DIM) & (lane < (hh + 1) * GROUP_DIM)
        qh = jnp.where(in_head, q, zero)
        sel_t = _top_blocks(_dot_nt(kmean, qh), own)
        sel = jnp.transpose(jnp.concatenate([sel_t, pad], axis=0))
        q_heads.append(qh)
        bias_heads.append(jnp.where(sel > 0.0, 0.0, -jnp.inf))

    for g in range(nblk // group):
        nk = (g + 1) * group

        @pl.when(step == g)
        def _(g=g, nk=nk):
            keys = k_ref[0, 0:nk * blk_rows, :]
            vals = v_ref[0, 0:nk * blk_rows, :]
            outs = []
            for qh, bias in zip(q_heads, bias_heads):
                s = _dot_nt(qh, keys)
                pieces = []
                for jb in range(nk):
                    if jb < g * group:
                        blk_bias = bias[:, jb:jb + 1]
                    else:
                        d = jb - g * group
                        blk_bias = jnp.concatenate(
                            [causal_bias if r == d else
                             (jnp.broadcast_to(bias[r * blk_rows:(r + 1) * blk_rows, jb:jb + 1], (blk_rows, blk_rows))
                              if r > d else future_bias)
                             for r in range(group)], axis=0)
                    pieces.append(s[:, jb * blk_rows:(jb + 1) * blk_rows] + blk_bias)
                s = jnp.concatenate(pieces, axis=1)
                m = jnp.max(s, axis=1, keepdims=True)
                p = jnp.exp(s - m)
                l = jnp.sum(p, axis=1, keepdims=True)
                outs.append(_dot(p.astype(BF16), vals) / l)
            o_ref[0] = jnp.where(lane < GROUP_DIM, outs[0], outs[1]).astype(o_ref.dtype)


def _moba(q, k, v, kmean, cast_weights):
    b, s, _ = q.shape
    assert s % (MOBA_KEY_GROUP * MOBA_BLOCK) == 0
    nb = s // (MOBA_KEY_GROUP * MOBA_BLOCK)
    pairs = MIX_W // LANES
    steps = b * pairs * nb
    q_spec = pl.BlockSpec((1, MOBA_KEY_GROUP * MOBA_BLOCK, LANES), lambda i, p, j: (i, j, p))
    kv_spec = pl.BlockSpec((1, s, LANES), lambda i, p, j: (i, 0, p))
    bf16_rows = 2 * SUBLANES
    sliced = [w.reshape(steps, -1, w.shape[-1]) for w in cast_weights]
    assert all(w.shape[1] % bf16_rows == 0 and w.shape[2] % LANES == 0 for w in sliced)
    w_specs = [pl.BlockSpec((1,) + w.shape[1:], lambda i, p, j: ((i * pairs + p) * nb + j, 0, 0)) for w in sliced]
    outs = pl.pallas_call(
        _moba_kernel,
        grid=(b, pairs, nb),
        in_specs=[q_spec, kv_spec, kv_spec,
                  pl.BlockSpec((1, s // MOBA_BLOCK, 1, LANES), lambda i, p, j: (i, 0, 0, p))] + w_specs,
        out_specs=[q_spec] + w_specs,
        out_shape=[jax.ShapeDtypeStruct(q.shape, BF16)] + [jax.ShapeDtypeStruct(w.shape, BF16) for w in sliced],
        compiler_params=_params("arbitrary", "arbitrary", "arbitrary"),
        name="moba_attention",
    )(q, k, v, kmean, *sliced)
    return outs[0], [o.reshape(w.shape) for o, w in zip(outs[1:], cast_weights)]


def _odd_out_kernel(yc_ref, yd_ref, x_ref, mod_ref, wout_ref, gain_ref, rwh_ref, rwl_ref, rb_ref,
                    x2_ref, h2_ref, wt_ref, mk_ref, cnt_ref):
    x = x_ref[0]
    mix = _dot(yc_ref[0], wout_ref[0:MIX_W, :]) + _dot(yd_ref[0], wout_ref[MIX_W:2 * MIX_W, :])
    x2 = x + mod_ref[0, 2:3, :] * mix
    x2_ref[0] = x2
    h = _rms_mod(x2, gain_ref[...], mod_ref[0, 4:5, :], mod_ref[0, 3:4, :])
    h_hi = h.astype(BF16)
    h2_ref[0] = h_hi.astype(h2_ref.dtype)

    h_lo = (h - h_hi.astype(F32)).astype(BF16)
    rwh = rwh_ref[...]
    logits = _dot(h_hi, rwh) + _dot(h_lo, rwh) + _dot(h_hi, rwl_ref[...]) + rb_ref[...]
    lane = lax.broadcasted_iota(jnp.int32, logits.shape, 1)
    valid = lane < N_EXPERTS
    lg = jnp.where(valid, logits, -jnp.inf)
    top1 = jnp.max(lg, axis=1, keepdims=True)
    idx1 = jnp.min(jnp.where(lg == top1, lane, LANES), axis=1, keepdims=True)
    pick1 = lane == idx1
    lg2 = jnp.where(pick1, -jnp.inf, lg)
    top2 = jnp.max(lg2, axis=1, keepdims=True)
    idx2 = jnp.min(jnp.where((lg2 == top2) & valid & jnp.logical_not(pick1), lane, LANES), axis=1, keepdims=True)
    pick2 = lane == idx2
    e2 = jnp.exp(top2 - top1)
    den = 1.0 + e2
    wt_ref[0] = jnp.where(pick1, 1.0 / den, 0.0) + jnp.where(pick2, e2 / den, 0.0)
    mask = jnp.where(pick1 | pick2, 1.0, 0.0)
    mk_ref[0] = mask.astype(mk_ref.dtype)
    cnt_ref[0, 0] = jnp.sum(mask, axis=0, keepdims=True)


def _odd_out(yc, yd, x, mod, w_out, gain, rw_hi, rw_lo, rb):
    b, s, d = x.shape
    rows = min(MIX_ROWS, s)
    row_spec = pl.BlockSpec((1, rows, d), lambda i, j: (i, j, 0))
    mix_spec = pl.BlockSpec((1, rows, MIX_W), lambda i, j: (i, j, 0))
    lane_spec = pl.BlockSpec((1, rows, LANES), lambda i, j: (i, j, 0))
    small = [w_out, gain, rw_hi, rw_lo, rb]
    return pl.pallas_call(
        _odd_out_kernel,
        grid=(b, s // rows),
        in_specs=[mix_spec, mix_spec, row_spec, pl.BlockSpec((1, 6, d), lambda i, j: (i, 0, 0))]
                 + [_full(t.shape) for t in small],
        out_specs=[row_spec, row_spec, lane_spec, lane_spec,
                   pl.BlockSpec((1, 1, 1, LANES), lambda i, j: (i, j, 0, 0))],
        out_shape=[jax.ShapeDtypeStruct(x.shape, F32),
                   jax.ShapeDtypeStruct(x.shape, BF16),
                   jax.ShapeDtypeStruct((b, s, LANES), F32),
                   jax.ShapeDtypeStruct((b, s, LANES), BF16),
                   jax.ShapeDtypeStruct((b, s // rows, 1, LANES), F32)],
        compiler_params=_params("arbitrary", "arbitrary"),
        name="odd_out_router",
    )(yc, yd, x, mod, *small)


def _moe_kernel(cnt_ref, cume_ref, h2_ref, wt_ref, mk_ref, x2_ref, mod_ref, wg_ref, wu_ref, wd_ref, o_ref,
                posc_scr, posr_scr, xs_scr, acc_scr, posb_scr, wb_scr):
    tile = pl.program_id(0)
    e = pl.program_id(1)
    f = pl.program_id(2)
    rows = h2_ref.shape[0]
    ch = MOE_CHUNK
    big, med, small = MOE_BIG_ROWS, MOE_MED_ROWS, MOE_SMALL_ROWS
    count = cnt_ref[tile * N_EXPERTS + e]
    units = lax.shift_right_logical(count + (small - 1), small.bit_length() - 1)
    n_big = lax.shift_right_logical(units, 2)
    n_med = lax.shift_right_logical(units & 3, 1)
    n_small = units & 1
    med_base = n_big * big
    small_base = med_base + n_med * med

    def for_blocks(body):
        lax.fori_loop(0, n_big, lambda kk, c: body(pl.multiple_of(kk * big, big), big) or c, 0)
        lax.fori_loop(0, n_med, lambda kk, c: body(pl.multiple_of(med_base, med), med) or c, 0)
        lax.fori_loop(0, n_small, lambda kk, c: body(pl.multiple_of(small_base, small), small) or c, 0)

    @pl.when((e == 0) & (f == 0))
    def _positions():
        o_ref[...] = jnp.zeros(o_ref.shape, F32)
        r_i = lax.broadcasted_iota(jnp.int32, (ch, ch), 0)
        c_i = lax.broadcasted_iota(jnp.int32, (ch, ch), 1)
        lower = jnp.where(c_i < r_i, 1.0, 0.0).astype(BF16)
        upper = jnp.where(r_i < c_i, 1.0, 0.0).astype(BF16)
        eye = jnp.where(lax.broadcasted_iota(jnp.int32, (N_EXPERTS, LANES), 0)
                        == lax.broadcasted_iota(jnp.int32, (N_EXPERTS, LANES), 1), 1.0, 0.0).astype(BF16)
        carry_c = jnp.zeros((1, LANES), F32)
        carry_r = jnp.zeros((N_EXPERTS, 1), F32)
        for blk in range(rows // ch):
            sl = slice(blk * ch, (blk + 1) * ch)
            mb = mk_ref[sl, :]
            mbf = mb.astype(F32)
            posc_scr[sl, :] = jnp.where(mbf > 0.0, _dot(lower, mb) + carry_c, -1.0)
            carry_c = carry_c + jnp.sum(mbf, axis=0, keepdims=True)
            mbt = _dot_nt(eye, mb)
            posr_scr[:, sl] = jnp.where(mbt > 0.0, _dot(mbt.astype(BF16), upper) + carry_r, -1.0)
            carry_r = carry_r + jnp.sum(mbt, axis=1, keepdims=True)

    sub = MOE_SCATTER_ROWS
    nsub = rows // sub
    win = MOE_WINDOW_ROWS
    cum_base = (tile * N_EXPERTS + e) * (nsub + 1)
    width = acc_scr.shape[1]

    def windows(tb):
        lo = cume_ref[cum_base + tb]
        hi = cume_ref[cum_base + tb + 1]
        start = lax.shift_left(lax.shift_right_logical(lo, small.bit_length() - 1), small.bit_length() - 1)
        n_win = lax.shift_right_logical(hi - start + (win - 1), win.bit_length() - 1)
        return start, jnp.where(hi > lo, n_win, 0)

    def unit_rows(u):
        return pl.ds(pl.multiple_of(u * small, small), small)

    @pl.when(f == 0)
    def _gather():
        def zero(u, c):
            acc_scr[unit_rows(u), :] = jnp.zeros((small, width), F32)
            return c

        lax.fori_loop(0, units + win // small, zero, 0)
        r_i = lax.broadcasted_iota(jnp.int32, (win, sub), 0).astype(F32)
        for tb in range(nsub):
            start, n_win = windows(tb)
            posr = posr_scr[pl.ds(e, 1), tb * sub:(tb + 1) * sub]
            h2_tb = h2_ref[tb * sub:(tb + 1) * sub, :]

            def gather_window(k, c, start=start, posr=posr, h2_tb=h2_tb):
                off = pl.multiple_of(start + k * win, small)
                onehot = jnp.where(posr - off.astype(F32) == r_i, 1.0, 0.0).astype(BF16)
                acc_scr[pl.ds(off, win), :] += _dot(onehot, h2_tb)
                return c

            lax.fori_loop(0, n_win, gather_window, 0)

        def to_bf16(u, c):
            xs_scr[unit_rows(u), :] = acc_scr[unit_rows(u), :].astype(xs_scr.dtype)
            acc_scr[unit_rows(u), :] = jnp.zeros((small, width), F32)
            return c

        lax.fori_loop(0, units, to_bf16, 0)

    def ffn_body(off, m):
        xk = xs_scr[pl.ds(off, m), :]
        g = _dot(xk, wg_ref[0])
        act = (g * _sigmoid(g)) * _dot(xk, wu_ref[0])
        acc_scr[pl.ds(off, m), :] += _dot(act.astype(BF16), wd_ref[0])

    for_blocks(ffn_body)

    @pl.when(f == pl.num_programs(2) - 1)
    def _scatter():
        lane = lax.broadcasted_iota(jnp.int32, (sub, LANES), 1)
        lane_f = lane.astype(F32)
        reps = width // LANES
        for tb in range(nsub):
            sl = slice(tb * sub, (tb + 1) * sub)
            pos_e = jnp.sum(jnp.where(lane == e, posc_scr[sl, :], 0.0), axis=1, keepdims=True)
            w_e = jnp.sum(jnp.where(lane == e, wt_ref[sl, :], 0.0), axis=1, keepdims=True)
            posb_scr[...] = jnp.broadcast_to(pos_e, (sub, LANES)) - lane_f
            wb_scr[...] = jnp.broadcast_to(w_e, (sub, LANES))
            start, n_win = windows(tb)

            def scatter_window(k, c, start=start, sl=sl):
                off = pl.multiple_of(start + k * win, small)
                y = acc_scr[pl.ds(off, win), :].astype(BF16)
                rel = posb_scr[...] - off.astype(F32)
                onehot = jnp.concatenate(
                    [jnp.where(rel == float(g * LANES), 1.0, 0.0) for g in range(win // LANES)], axis=1).astype(BF16)
                o_ref[sl, :] += jnp.concatenate([wb_scr[...]] * reps, axis=1) * _dot(onehot, y)
                return c

            lax.fori_loop(0, n_win, scatter_window, 0)

        @pl.when(e == N_EXPERTS - 1)
        def _residual():
            o_ref[...] = x2_ref[...] + mod_ref[0, 5:6, :] * o_ref[...]


def _moe(h2, wts, mask, x2, mod, counts, cum_counts, wg, wu, wd, seq):
    t, d = h2.shape
    fdim = wg.shape[2]
    cols = MOE_COLS if fdim % MOE_COLS == 0 else fdim
    n_f = fdim // cols
    rows = min(MOE_ROWS, seq)
    once = pl.Buffered(1)
    tile_spec = lambda width: pl.BlockSpec((rows, width), lambda i, e, f, n, c: (i, 0), pipeline_mode=once)
    grid_spec = pltpu.PrefetchScalarGridSpec(
        num_scalar_prefetch=2,
        grid=(t // rows, N_EXPERTS, n_f),
        in_specs=[tile_spec(d), tile_spec(LANES), tile_spec(LANES), tile_spec(d),
                  pl.BlockSpec((1, 6, d), lambda i, e, f, n, c: (i * rows // seq, 0, 0)),
                  pl.BlockSpec((1, d, cols), lambda i, e, f, n, c: (e, 0, f)),
                  pl.BlockSpec((1, d, cols), lambda i, e, f, n, c: (e, 0, f)),
                  pl.BlockSpec((1, cols, d), lambda i, e, f, n, c: (e, f, 0))],
        out_specs=pl.BlockSpec((rows, d), lambda i, e, f, n, c: (i, 0), pipeline_mode=once),
        scratch_shapes=[pltpu.VMEM((rows, LANES), F32),
                        pltpu.VMEM((N_EXPERTS, rows), F32),
                        pltpu.VMEM((rows, d), BF16),
                        pltpu.VMEM((rows + MOE_WINDOW_ROWS, d), F32),
                        pltpu.VMEM((MOE_SCATTER_ROWS, LANES), F32),
                        pltpu.VMEM((MOE_SCATTER_ROWS, LANES), F32)],
    )
    return pl.pallas_call(
        _moe_kernel,
        grid_spec=grid_spec,
        out_shape=jax.ShapeDtypeStruct((t, d), F32),
        compiler_params=_params("arbitrary", "arbitrary", "arbitrary"),
        name="moe_experts",
    )(counts, cum_counts, h2, wts, mask, x2, mod, wg, wu, wd)


def _block_diag(w):
    h, i, j = w.shape
    eye = jnp.eye(h, dtype=w.dtype)
    return jnp.einsum('hij,hk->hikj', w, eye).reshape(h * i, h * j)


def _rope_tables(seq):
    half = GROUP_DIM // 2
    inv = ROPE_THETA ** (-jnp.arange(half, dtype=F32) / half)
    ang = jnp.arange(seq).astype(F32)[:, None] * inv[None, :]
    cos, sin, zero = jnp.cos(ang), jnp.sin(ang), jnp.zeros_like(ang)
    reps = LANES // GROUP_DIM
    return (jnp.tile(jnp.concatenate([cos, cos], axis=1), (1, reps)),
            jnp.tile(jnp.concatenate([-sin, zero], axis=1), (1, reps)),
            jnp.tile(jnp.concatenate([zero, sin], axis=1), (1, reps)))


def _row(v):
    return v.reshape(1, -1).astype(F32)


def kernel(x, c, e_ada_w, e_ada_b, e_norm_mix, e_norm_ffn, e_w_in, e_conv_a_w, e_conv_a_b,
           e_ln_a_g, e_ln_a_b, e_conv_b_w, e_conv_b_b, e_lru_wa, e_lru_ba, e_lru_wx, e_lru_bx,
           e_lru_lambda, e_w_out, e_ffn_wg, e_ffn_wu, e_ffn_wd,
           o_ada_w, o_ada_b, o_norm_mix, o_norm_ffn, o_w_in, o_conv_c_w, o_q_norm, o_k_norm,
           o_w_out, o_router_w, o_router_b, o_moe_wg, o_moe_wu, o_moe_wd):
    b, s, d = x.shape
    assert s % MOBA_BLOCK == 0 and d % LANES == 0
    c_pad = jnp.zeros((SUBLANES, d), F32).at[:b].set(c.astype(F32))
    seg = jnp.kron(jnp.eye(N_GROUPS, dtype=F32), jnp.ones((GROUP_DIM, GROUP_DIM), F32)).astype(BF16)
    cos, sin_a, sin_b = _rope_tables(s)

    def modulation(w, bias):
        return _modulation(c_pad, w, _row(bias))[:b].reshape(b, 6, d)

    x = x.astype(F32)
    for layer in range(DEPTH):
        j = layer // 2
        if layer % 2 == 0:
            mod = modulation(e_ada_w[j], e_ada_b[j])
            x = _even_mixer(
                x, mod, _row(e_norm_mix[j]), e_w_in[j].astype(BF16),
                e_conv_a_w[j].astype(F32), _row(e_conv_a_b[j]), _row(e_ln_a_g[j]), _row(e_ln_a_b[j]),
                e_conv_b_w[j].astype(F32), _row(e_conv_b_b[j]),
                _block_diag(e_lru_wa[j]).astype(BF16), _row(e_lru_ba[j]),
                _block_diag(e_lru_wx[j]).astype(BF16), _row(e_lru_bx[j]),
                _row(e_lru_lambda[j]), e_w_out[j].astype(BF16))
            x = _dense_ffn(x, mod, _row(e_norm_ffn[j]), e_ffn_wg[j].astype(BF16),
                           e_ffn_wu[j].astype(BF16), e_ffn_wd[j].astype(BF16))
        else:
            mod = modulation(o_ada_w[j], o_ada_b[j])
            yc, q, k, v, kmean = _odd_mixer(
                x, mod, _row(o_norm_mix[j]), o_w_in[j].astype(BF16), o_conv_c_w[j].astype(F32),
                _row(jnp.tile(o_q_norm[j], N_GROUPS)), _row(jnp.tile(o_k_norm[j], N_GROUPS)),
                seg, cos, sin_a, sin_b)
            yd, (moe_wg, moe_wu, moe_wd) = _moba(
                q, k, v, kmean, [o_moe_wg[j].astype(F32), o_moe_wu[j].astype(F32), o_moe_wd[j].astype(F32)])
            rw = jnp.zeros((d, LANES), F32).at[:, :N_EXPERTS].set(o_router_w[j].astype(F32))
            rw_hi = rw.astype(BF16)
            rw_lo = (rw - rw_hi.astype(F32)).astype(BF16)
            rb = jnp.zeros((1, LANES), F32).at[0, :N_EXPERTS].set(o_router_b[j].astype(F32))
            x2, h2, wts, mask, cnt = _odd_out(yc, yd, x, mod, o_w_out[j].astype(BF16),
                                              _row(o_norm_ffn[j]), rw_hi, rw_lo, rb)
            rows = min(MOE_ROWS, s)
            assert MIX_ROWS == MOE_SCATTER_ROWS or s < MIX_ROWS
            sub_cnt = cnt.reshape(b * s // rows, -1, LANES)[:, :, :N_EXPERTS].astype(jnp.int32)
            cum = jnp.cumsum(sub_cnt, axis=1)
            cum = jnp.concatenate([jnp.zeros_like(cum[:, :1]), cum], axis=1)
            counts = cum[:, -1, :].reshape(-1)
            cum_counts = jnp.transpose(cum, (0, 2, 1)).reshape(-1)
            out = _moe(h2.reshape(b * s, d), wts.reshape(b * s, LANES), mask.reshape(b * s, LANES),
                       x2.reshape(b * s, d), mod, counts, cum_counts,
                       moe_wg, moe_wu, moe_wd, s)
            x = out.reshape(b, s, d)
    return x
```

```python
import functools

import jax
import jax.numpy as jnp
from jax import lax
from jax.experimental import pallas as pl
from jax.experimental.pallas import tpu as pltpu

F32 = jnp.float32
BF16 = jnp.bfloat16

N_GROUPS = 8
GROUP_DIM = 64
MIX_W = N_GROUPS * GROUP_DIM
CONF_WIDTH = 31
LRU_CONV_WIDTH = 4
LRU_C = 8.0
SHORT_CONV_WIDTH = 3
MOBA_BLOCK = 256
MOBA_TOPK = 3
ROPE_THETA = 10000.0
N_EXPERTS = 8
EPS = 1e-6
DEPTH = 2

LANES = 128
SUBLANES = 8
VMEM_LIMIT_BYTES = 62 * 1024 * 1024

MIX_ROWS = 512
FFN_ROWS = 512
FFN_COLS = 2816
MOE_ROWS = 2048
MOE_COLS = 1792
MOE_CHUNK = 256
MOE_BIG_ROWS = 512
MOE_MED_ROWS = 256
MOE_SMALL_ROWS = 128
MOE_SCATTER_ROWS = 512
MOE_WINDOW_ROWS = 256
MOBA_KEY_GROUP = 2
CONV_A_HALO = 32
CONV_S_HALO = 8


def _dot(a, b):
    return jnp.dot(a, b, preferred_element_type=F32)


def _dot_nt(a, b):
    return lax.dot_general(a, b, (((1,), (1,)), ((), ())), preferred_element_type=F32)


def _sigmoid(x):
    return 1.0 / (1.0 + jnp.exp(-x))


def _rms_mod(x, gain, scale, shift):
    ms = jnp.mean(x * x, axis=-1, keepdims=True)
    return (x * lax.rsqrt(ms + EPS)) * gain * (1.0 + scale) + shift


def _params(*sem):
    return pltpu.CompilerParams(dimension_semantics=sem, vmem_limit_bytes=VMEM_LIMIT_BYTES)


def _full(shape):
    n = len(shape)
    return pl.BlockSpec(shape, lambda *_: (0,) * n)


def _mod_kernel(c_ref, w_ref, b_ref, o_ref):
    c = c_ref[...]
    c_act = c * _sigmoid(c)
    o_ref[...] = _dot(c_act.astype(BF16), w_ref[...].astype(BF16)) + b_ref[...]


def _modulation(c_pad, w, b):
    d, n = w.shape
    tn = n // 4
    return pl.pallas_call(
        _mod_kernel,
        grid=(4,),
        in_specs=[_full(c_pad.shape),
                  pl.BlockSpec((d, tn), lambda j: (0, j)),
                  pl.BlockSpec((1, tn), lambda j: (0, j))],
        out_specs=pl.BlockSpec((c_pad.shape[0], tn), lambda j: (0, j)),
        out_shape=jax.ShapeDtypeStruct((c_pad.shape[0], n), F32),
        compiler_params=_params("arbitrary"),
        name="adaln_mod",
    )(c_pad, w, b)


def _causal_conv(buf, halo, rows, w_ref, width):
    total = halo + rows
    full = buf[0:total, :]
    acc = None
    for phase in range(SUBLANES):
        taps = [j for j in range(width) if (halo - (width - 1) + j) % SUBLANES == phase]
        if not taps:
            continue
        shifted = full if phase == 0 else pltpu.roll(full, total - phase, 0)
        for j in taps:
            start = halo - (width - 1) + j - phase
            term = w_ref[j:j + 1, :] * shifted[start:start + rows, :]
            acc = term if acc is None else acc + term
    return acc


def _even_mixer_kernel(x_ref, mod_ref, gain_ref, win_ref, caw_ref, cab_ref, lng_ref, lnb_ref,
                       cbw_ref, cbb_ref, wa_ref, ba_ref, wx_ref, bx_ref, lam_ref, wout_ref,
                       o_ref, abuf, bbuf, hcar):
    rows = x_ref.shape[1]

    @pl.when(pl.program_id(1) == 0)
    def _():
        abuf[0:CONV_A_HALO, :] = jnp.zeros((CONV_A_HALO, MIX_W), F32)
        bbuf[0:CONV_S_HALO, :] = jnp.zeros((CONV_S_HALO, MIX_W), F32)
        hcar[...] = jnp.zeros(hcar.shape, F32)

    x = x_ref[0]
    h = _rms_mod(x, gain_ref[...], mod_ref[0, 1:2, :], mod_ref[0, 0:1, :])
    u = _dot(h.astype(BF16), win_ref[...])
    a_val = u[:, 0 * MIX_W:1 * MIX_W]
    a_gate = u[:, 1 * MIX_W:2 * MIX_W]
    b_x = u[:, 2 * MIX_W:3 * MIX_W]
    b_gate = u[:, 3 * MIX_W:4 * MIX_W]

    abuf[CONV_A_HALO:CONV_A_HALO + rows, :] = a_val * _sigmoid(a_gate)
    ya = _causal_conv(abuf, CONV_A_HALO, rows, caw_ref, CONF_WIDTH) + cab_ref[...]
    abuf[0:CONV_A_HALO, :] = abuf[rows:rows + CONV_A_HALO, :]
    mu = jnp.mean(ya, axis=-1, keepdims=True)
    dev = ya - mu
    var = jnp.mean(dev * dev, axis=-1, keepdims=True)
    ya = dev * lax.rsqrt(var + EPS) * lng_ref[...] + lnb_ref[...]
    ya = ya * _sigmoid(ya)

    bbuf[CONV_S_HALO:CONV_S_HALO + rows, :] = b_x
    xb = _causal_conv(bbuf, CONV_S_HALO, rows, cbw_ref, LRU_CONV_WIDTH) + cbb_ref[...]
    bbuf[0:CONV_S_HALO, :] = bbuf[rows:rows + CONV_S_HALO, :]
    xb16 = xb.astype(BF16)
    r_gate = _sigmoid(_dot(xb16, wa_ref[...]) + ba_ref[...])
    i_gate = _sigmoid(_dot(xb16, wx_ref[...]) + bx_ref[...])
    lam = lam_ref[...]
    log_sig = -(jnp.maximum(-lam, 0.0) + jnp.log1p(jnp.exp(-jnp.abs(lam))))
    log_a = LRU_C * r_gate * log_sig
    a = jnp.exp(log_a)
    mult = jnp.sqrt(-jnp.tanh(log_a) * (1.0 + a * a))
    bterm = mult * (i_gate * xb)

    row = lax.broadcasted_iota(jnp.int32, (rows, MIX_W), 0)
    d = 1
    while d < rows:
        keep = row >= d
        a_prev = jnp.where(keep, pltpu.roll(a, d, 0), 1.0)
        b_prev = jnp.where(keep, pltpu.roll(bterm, d, 0), 0.0)
        bterm = bterm + a * b_prev
        a = a * a_prev
        d *= 2
    hseq = bterm + a * hcar[0:1, :]
    hcar[...] = jnp.broadcast_to(hseq[rows - 1:rows, :], hcar.shape)

    gelu = 0.5 * b_gate * (1.0 + jnp.tanh(0.7978845608028654 * (b_gate + 0.044715 * (b_gate * b_gate * b_gate))))
    yb = hseq * gelu

    mix = _dot(ya.astype(BF16), wout_ref[0:MIX_W, :]) + _dot(yb.astype(BF16), wout_ref[MIX_W:2 * MIX_W, :])
    o_ref[0] = x + mod_ref[0, 2:3, :] * mix


def _even_mixer(x, mod, gain, w_in, caw, cab, lng, lnb, cbw, cbb, wa, ba, wx, bx, lam, w_out):
    b, s, d = x.shape
    rows = min(MIX_ROWS, s)
    row_spec = pl.BlockSpec((1, rows, d), lambda i, j: (i, j, 0))
    small = [gain, w_in, caw, cab, lng, lnb, cbw, cbb, wa, ba, wx, bx, lam, w_out]
    return pl.pallas_call(
        _even_mixer_kernel,
        grid=(b, s // rows),
        in_specs=[row_spec, pl.BlockSpec((1, 6, d), lambda i, j: (i, 0, 0))] + [_full(t.shape) for t in small],
        out_specs=row_spec,
        out_shape=jax.ShapeDtypeStruct(x.shape, F32),
        scratch_shapes=[pltpu.VMEM((CONV_A_HALO + rows, MIX_W), F32),
                        pltpu.VMEM((CONV_S_HALO + rows, MIX_W), F32),
                        pltpu.VMEM((SUBLANES, MIX_W), F32)],
        compiler_params=_params("arbitrary", "arbitrary"),
        name="even_mixer",
    )(x, mod, *small)


def _ffn_kernel(x_ref, mod_ref, gain_ref, wg_ref, wu_ref, wd_ref, o_ref, h_scr):
    @pl.when(pl.program_id(2) == 0)
    def _():
        x = x_ref[0]
        h_scr[...] = _rms_mod(x, gain_ref[...], mod_ref[0, 4:5, :], mod_ref[0, 3:4, :]).astype(BF16)
        o_ref[0] = x

    h = h_scr[...]
    g = _dot(h, wg_ref[...])
    act = (g * _sigmoid(g)) * _dot(h, wu_ref[...])
    o_ref[0] += mod_ref[0, 5:6, :] * _dot(act.astype(BF16), wd_ref[...])


def _dense_ffn(x, mod, gain, wg, wu, wd):
    b, s, d = x.shape
    f = wg.shape[1]
    rows = min(FFN_ROWS, s)
    cols = FFN_COLS if f % FFN_COLS == 0 else f
    row_spec = pl.BlockSpec((1, rows, d), lambda i, j, k: (i, j, 0))
    return pl.pallas_call(
        _ffn_kernel,
        grid=(b, s // rows, f // cols),
        in_specs=[row_spec,
                  pl.BlockSpec((1, 6, d), lambda i, j, k: (i, 0, 0)),
                  _full(gain.shape),
                  pl.BlockSpec((d, cols), lambda i, j, k: (0, k)),
                  pl.BlockSpec((d, cols), lambda i, j, k: (0, k)),
                  pl.BlockSpec((cols, d), lambda i, j, k: (k, 0))],
        out_specs=row_spec,
        out_shape=jax.ShapeDtypeStruct(x.shape, F32),
        scratch_shapes=[pltpu.VMEM((rows, d), BF16)],
        compiler_params=_params("arbitrary", "arbitrary", "arbitrary"),
        name="dense_ffn",
    )(x, mod, gain, wg, wu, wd)


def _odd_mixer_kernel(x_ref, mod_ref, gain_ref, win_ref, ccw_ref, qn_ref, kn_ref, seg_ref,
                      cos_ref, sa_ref, sb_ref, yc_ref, q_ref, k_ref, v_ref, km_ref, cbuf):
    rows = x_ref.shape[1]

    @pl.when(pl.program_id(1) == 0)
    def _():
        cbuf[0:CONV_S_HALO, :] = jnp.zeros((CONV_S_HALO, MIX_W), F32)

    x = x_ref[0]
    h = _rms_mod(x, gain_ref[...], mod_ref[0, 1:2, :], mod_ref[0, 0:1, :])
    u = _dot(h.astype(BF16), win_ref[...])
    c_h = u[:, 0 * MIX_W:1 * MIX_W]
    c_b = u[:, 1 * MIX_W:2 * MIX_W]
    c_c = u[:, 2 * MIX_W:3 * MIX_W]
    q = u[:, 3 * MIX_W:4 * MIX_W]
    k = u[:, 4 * MIX_W:5 * MIX_W]
    v = u[:, 5 * MIX_W:6 * MIX_W]

    cbuf[CONV_S_HALO:CONV_S_HALO + rows, :] = c_c * c_h
    conv = _causal_conv(cbuf, CONV_S_HALO, rows, ccw_ref, SHORT_CONV_WIDTH)
    cbuf[0:CONV_S_HALO, :] = cbuf[rows:rows + CONV_S_HALO, :]
    yc_ref[0] = (c_b * conv).astype(yc_ref.dtype)

    seg = seg_ref[...]
    cos = cos_ref[...]
    sin_a = sa_ref[...]
    sin_b = sb_ref[...]

    def head_norm_rope(t, gain):
        t2 = t * t
        hi = t2.astype(BF16)
        lo = (t2 - hi.astype(F32)).astype(BF16)
        ssq = _dot(hi, seg) + _dot(lo, seg)
        t = t * lax.rsqrt(ssq * (1.0 / GROUP_DIM) + EPS) * gain
        outs = []
        for g in range(MIX_W // LANES):
            tg = t[:, g * LANES:(g + 1) * LANES]
            outs.append(tg * cos + pltpu.roll(tg, LANES - GROUP_DIM // 2, 1) * sin_a
                        + pltpu.roll(tg, GROUP_DIM // 2, 1) * sin_b)
        return jnp.concatenate(outs, axis=1)

    qr = head_norm_rope(q, qn_ref[...])
    kr = head_norm_rope(k, kn_ref[...])
    q_ref[0] = (qr * (GROUP_DIM ** -0.5)).astype(q_ref.dtype)
    k_ref[0] = kr.astype(k_ref.dtype)
    v_ref[0] = v.astype(v_ref.dtype)
    for i in range(rows // MOBA_BLOCK):
        km_ref[0, i] = jnp.mean(kr[i * MOBA_BLOCK:(i + 1) * MOBA_BLOCK, :], axis=0, keepdims=True)


def _odd_mixer(x, mod, gain, w_in, ccw, qn, kn, seg, cos, sin_a, sin_b):
    b, s, d = x.shape
    rows = min(MIX_ROWS, s)
    nb = s // MOBA_BLOCK
    row_spec = pl.BlockSpec((1, rows, d), lambda i, j: (i, j, 0))
    mix_spec = pl.BlockSpec((1, rows, MIX_W), lambda i, j: (i, j, 0))
    tab_spec = pl.BlockSpec((rows, LANES), lambda i, j: (j, 0))
    small = [gain, w_in, ccw, qn, kn, seg]
    mix_shape = jax.ShapeDtypeStruct((b, s, MIX_W), BF16)
    return pl.pallas_call(
        _odd_mixer_kernel,
        grid=(b, s // rows),
        in_specs=[row_spec, pl.BlockSpec((1, 6, d), lambda i, j: (i, 0, 0))]
                 + [_full(t.shape) for t in small] + [tab_spec, tab_spec, tab_spec],
        out_specs=[mix_spec, mix_spec, mix_spec, mix_spec,
                   pl.BlockSpec((1, rows // MOBA_BLOCK, 1, MIX_W), lambda i, j: (i, j, 0, 0))],
        out_shape=[mix_shape, mix_shape, mix_shape, mix_shape,
                   jax.ShapeDtypeStruct((b, nb, 1, MIX_W), F32)],
        scratch_shapes=[pltpu.VMEM((CONV_S_HALO + rows, MIX_W), F32)],
        compiler_params=_params("arbitrary", "arbitrary"),
        name="odd_mixer",
    )(x, mod, *small, cos, sin_a, sin_b)


def _top_blocks(gate, own):
    nblk = gate.shape[0]
    blk = lax.broadcasted_iota(jnp.int32, gate.shape, 0)
    remaining = blk < own
    sel = jnp.zeros(gate.shape, F32)
    for _ in range(MOBA_TOPK):
        gm = jnp.where(remaining, gate, -jnp.inf)
        top = jnp.max(gm, axis=0, keepdims=True)
        cand = remaining & (gm == top)
        first = jnp.min(jnp.where(cand, blk, nblk), axis=0, keepdims=True)
        pick = cand & (blk == first)
        sel = jnp.where(pick, 1.0, sel)
        remaining = remaining & jnp.logical_not(pick)
    return sel


def _moba_kernel(q_ref, k_ref, v_ref, km_ref, *rest):
    n_cast = (len(rest) - 1) // 2
    o_ref = rest[n_cast]
    for src, dst in zip(rest[:n_cast], rest[n_cast + 1:]):
        dst[...] = src[...].astype(dst.dtype)
    group = MOBA_KEY_GROUP
    step = pl.program_id(2)
    nblk = km_ref.shape[1]
    blk_rows = MOBA_BLOCK
    q = q_ref[0]
    q_rows = q.shape[0]
    lane = lax.broadcasted_iota(jnp.int32, q.shape, 1)
    zero = jnp.zeros_like(q)
    kmean = km_ref[0, :, 0, :].astype(BF16)
    row = lax.broadcasted_iota(jnp.int32, (blk_rows, blk_rows), 0)
    col = lax.broadcasted_iota(jnp.int32, (blk_rows, blk_rows), 1)
    causal_bias = jnp.where(col <= row, 0.0, -jnp.inf)
    pad = jnp.zeros((LANES - nblk, q_rows), F32)
    q_pos = lax.broadcasted_iota(jnp.int32, (1, q_rows), 1)
    own = step * group + lax.shift_right_logical(q_pos, blk_rows.bit_length() - 1)

    q_heads, bias_heads = [], []
    for hh in range(LANES // GROUP_DIM):
        in_head = (lane >= hh * GROUP_DIM) & (lane < (hh + 1) * GROUP_DIM)
        qh = jnp.where(in_head, q, zero)
        sel_t = _top_blocks(_dot_nt(kmean, qh), own)
        sel = jnp.transpose(jnp.concatenate([sel_t, pad], axis=0))
        q_heads.append(qh)
        bias_heads.append(jnp.where(sel > 0.0, 0.0, -jnp.inf))

    for g in range(nblk // group):
        @pl.when(step == g)
        def _(g=g):
            outs = []
            for qh, bias in zip(q_heads, bias_heads):
                blocks = []
                for r in range(group):
                    nk = g * group + r + 1
                    q_sl = slice(r * blk_rows, (r + 1) * blk_rows)
                    s = _dot_nt(qh[q_sl, :], k_ref[0, 0:nk * blk_rows, :])
                    pieces = [s[:, jb * blk_rows:(jb + 1) * blk_rows] + bias[q_sl, jb:jb + 1] for jb in range(nk - 1)]
                    pieces.append(s[:, (nk - 1) * blk_rows:] + causal_bias)
                    s = jnp.concatenate(pieces, axis=1)
                    m = jnp.max(s, axis=1, keepdims=True)
                    p = jnp.exp(s - m)
                    l = jnp.sum(p, axis=1, keepdims=True)
                    blocks.append(_dot(p.astype(BF16), v_ref[0, 0:nk * blk_rows, :]) / l)
                outs.append(jnp.concatenate(blocks, axis=0))
            o_ref[0] = jnp.where(lane < GROUP_DIM, outs[0], outs[1]).astype(o_ref.dtype)


def _moba(q, k, v, kmean, cast_weights):
    b, s, _ = q.shape
    assert s % (MOBA_KEY_GROUP * MOBA_BLOCK) == 0
    nb = s // (MOBA_KEY_GROUP * MOBA_BLOCK)
    pairs = MIX_W // LANES
    steps = b * pairs * nb
    q_spec = pl.BlockSpec((1, MOBA_KEY_GROUP * MOBA_BLOCK, LANES), lambda i, p, j: (i, j, p))
    kv_spec = pl.BlockSpec((1, s, LANES), lambda i, p, j: (i, 0, p))
    bf16_rows = 2 * SUBLANES
    sliced = [w.reshape(steps, -1, w.shape[-1]) for w in cast_weights]
    assert all(w.shape[1] % bf16_rows == 0 and w.shape[2] % LANES == 0 for w in sliced)
    w_specs = [pl.BlockSpec((1,) + w.shape[1:], lambda i, p, j: ((i * pairs + p) * nb + j, 0, 0)) for w in sliced]
    outs = pl.pallas_call(
        _moba_kernel,
        grid=(b, pairs, nb),
        in_specs=[q_spec, kv_spec, kv_spec,
                  pl.BlockSpec((1, s // MOBA_BLOCK, 1, LANES), lambda i, p, j: (i, 0, 0, p))] + w_specs,
        out_specs=[q_spec] + w_specs,
        out_shape=[jax.ShapeDtypeStruct(q.shape, BF16)] + [jax.ShapeDtypeStruct(w.shape, BF16) for w in sliced],
        compiler_params=_params("arbitrary", "arbitrary", "arbitrary"),
        name="moba_attention",
    )(q, k, v, kmean, *sliced)
    return outs[0], [o.reshape(w.shape) for o, w in zip(outs[1:], cast_weights)]


def _odd_out_kernel(yc_ref, yd_ref, x_ref, mod_ref, wout_ref, gain_ref, rwh_ref, rwl_ref, rb_ref,
                    x2_ref, h2_ref, wt_ref, mk_ref, cnt_ref):
    x = x_ref[0]
    mix = _dot(yc_ref[0], wout_ref[0:MIX_W, :]) + _dot(yd_ref[0], wout_ref[MIX_W:2 * MIX_W, :])
    x2 = x + mod_ref[0, 2:3, :] * mix
    x2_ref[0] = x2
    h = _rms_mod(x2, gain_ref[...], mod_ref[0, 4:5, :], mod_ref[0, 3:4, :])
    h_hi = h.astype(BF16)
    h2_ref[0] = h_hi.astype(h2_ref.dtype)

    h_lo = (h - h_hi.astype(F32)).astype(BF16)
    rwh = rwh_ref[...]
    logits = _dot(h_hi, rwh) + _dot(h_lo, rwh) + _dot(h_hi, rwl_ref[...]) + rb_ref[...]
    lane = lax.broadcasted_iota(jnp.int32, logits.shape, 1)
    valid = lane < N_EXPERTS
    lg = jnp.where(valid, logits, -jnp.inf)
    top1 = jnp.max(lg, axis=1, keepdims=True)
    idx1 = jnp.min(jnp.where(lg == top1, lane, LANES), axis=1, keepdims=True)
    pick1 = lane == idx1
    lg2 = jnp.where(pick1, -jnp.inf, lg)
    top2 = jnp.max(lg2, axis=1, keepdims=True)
    idx2 = jnp.min(jnp.where((lg2 == top2) & valid & jnp.logical_not(pick1), lane, LANES), axis=1, keepdims=True)
    pick2 = lane == idx2
    e2 = jnp.exp(top2 - top1)
    den = 1.0 + e2
    wt_ref[0] = jnp.where(pick1, 1.0 / den, 0.0) + jnp.where(pick2, e2 / den, 0.0)
    mask = jnp.where(pick1 | pick2, 1.0, 0.0)
    mk_ref[0] = mask.astype(mk_ref.dtype)
    cnt_ref[0, 0] = jnp.sum(mask, axis=0, keepdims=True)


def _odd_out(yc, yd, x, mod, w_out, gain, rw_hi, rw_lo, rb):
    b, s, d = x.shape
    rows = min(MIX_ROWS, s)
    row_spec = pl.BlockSpec((1, rows, d), lambda i, j: (i, j, 0))
    mix_spec = pl.BlockSpec((1, rows, MIX_W), lambda i, j: (i, j, 0))
    lane_spec = pl.BlockSpec((1, rows, LANES), lambda i, j: (i, j, 0))
    small = [w_out, gain, rw_hi, rw_lo, rb]
    return pl.pallas_call(
        _odd_out_kernel,
        grid=(b, s // rows),
        in_specs=[mix_spec, mix_spec, row_spec, pl.BlockSpec((1, 6, d), lambda i, j: (i, 0, 0))]
                 + [_full(t.shape) for t in small],
        out_specs=[row_spec, row_spec, lane_spec, lane_spec,
                   pl.BlockSpec((1, 1, 1, LANES), lambda i, j: (i, j, 0, 0))],
        out_shape=[jax.ShapeDtypeStruct(x.shape, F32),
                   jax.ShapeDtypeStruct(x.shape, BF16),
                   jax.ShapeDtypeStruct((b, s, LANES), F32),
                   jax.ShapeDtypeStruct((b, s, LANES), BF16),
                   jax.ShapeDtypeStruct((b, s // rows, 1, LANES), F32)],
        compiler_params=_params("arbitrary", "arbitrary"),
        name="odd_out_router",
    )(yc, yd, x, mod, *small)


def _moe_kernel(cnt_ref, cume_ref, h2_ref, wt_ref, mk_ref, x2_ref, mod_ref, wg_ref, wu_ref, wd_ref, o_ref,
                posc_scr, posr_scr, xs_scr, acc_scr, posb_scr, wb_scr):
    tile = pl.program_id(0)
    e = pl.program_id(1)
    f = pl.program_id(2)
    rows = h2_ref.shape[0]
    ch = MOE_CHUNK
    big, med, small = MOE_BIG_ROWS, MOE_MED_ROWS, MOE_SMALL_ROWS
    count = cnt_ref[tile * N_EXPERTS + e]
    units = lax.shift_right_logical(count + (small - 1), small.bit_length() - 1)
    n_big = lax.shift_right_logical(units, 2)
    n_med = lax.shift_right_logical(units & 3, 1)
    n_small = units & 1
    med_base = n_big * big
    small_base = med_base + n_med * med

    def for_blocks(body):
        lax.fori_loop(0, n_big, lambda kk, c: body(pl.multiple_of(kk * big, big), big) or c, 0)
        lax.fori_loop(0, n_med, lambda kk, c: body(pl.multiple_of(med_base, med), med) or c, 0)
        lax.fori_loop(0, n_small, lambda kk, c: body(pl.multiple_of(small_base, small), small) or c, 0)

    @pl.when((e == 0) & (f == 0))
    def _positions():
        o_ref[...] = jnp.zeros(o_ref.shape, F32)
        r_i = lax.broadcasted_iota(jnp.int32, (ch, ch), 0)
        c_i = lax.broadcasted_iota(jnp.int32, (ch, ch), 1)
        lower = jnp.where(c_i < r_i, 1.0, 0.0).astype(BF16)
        upper = jnp.where(r_i < c_i, 1.0, 0.0).astype(BF16)
        eye = jnp.where(lax.broadcasted_iota(jnp.int32, (N_EXPERTS, LANES), 0)
                        == lax.broadcasted_iota(jnp.int32, (N_EXPERTS, LANES), 1), 1.0, 0.0).astype(BF16)
        carry_c = jnp.zeros((1, LANES), F32)
        carry_r = jnp.zeros((N_EXPERTS, 1), F32)
        for blk in range(rows // ch):
            sl = slice(blk * ch, (blk + 1) * ch)
            mb = mk_ref[sl, :]
            mbf = mb.astype(F32)
            posc_scr[sl, :] = jnp.where(mbf > 0.0, _dot(lower, mb) + carry_c, -1.0)
            carry_c = carry_c + jnp.sum(mbf, axis=0, keepdims=True)
            mbt = _dot_nt(eye, mb)
            posr_scr[:, sl] = jnp.where(mbt > 0.0, _dot(mbt.astype(BF16), upper) + carry_r, -1.0)
            carry_r = carry_r + jnp.sum(mbt, axis=1, keepdims=True)

    sub = MOE_SCATTER_ROWS
    nsub = rows // sub
    win = MOE_WINDOW_ROWS
    cum_base = (tile * N_EXPERTS + e) * (nsub + 1)
    width = acc_scr.shape[1]

    def windows(tb):
        lo = cume_ref[cum_base + tb]
        hi = cume_ref[cum_base + tb + 1]
        start = lax.shift_left(lax.shift_right_logical(lo, small.bit_length() - 1), small.bit_length() - 1)
        n_win = lax.shift_right_logical(hi - start + (win - 1), win.bit_length() - 1)
        return start, jnp.where(hi > lo, n_win, 0)

    def unit_rows(u):
        return pl.ds(pl.multiple_of(u * small, small), small)

    @pl.when(f == 0)
    def _gather():
        def zero(u, c):
            acc_scr[unit_rows(u), :] = jnp.zeros((small, width), F32)
            return c

        lax.fori_loop(0, units + win // small, zero, 0)
        r_i = lax.broadcasted_iota(jnp.int32, (win, sub), 0).astype(F32)
        for tb in range(nsub):
            start, n_win = windows(tb)
            posr = posr_scr[pl.ds(e, 1), tb * sub:(tb + 1) * sub]
            h2_tb = h2_ref[tb * sub:(tb + 1) * sub, :]

            def gather_window(k, c, start=start, posr=posr, h2_tb=h2_tb):
                off = pl.multiple_of(start + k * win, small)
                onehot = jnp.where(posr - off.astype(F32) == r_i, 1.0, 0.0).astype(BF16)
                acc_scr[pl.ds(off, win), :] += _dot(onehot, h2_tb)
                return c

            lax.fori_loop(0, n_win, gather_window, 0)

        def to_bf16(u, c):
            xs_scr[unit_rows(u), :] = acc_scr[unit_rows(u), :].astype(xs_scr.dtype)
            acc_scr[unit_rows(u), :] = jnp.zeros((small, width), F32)
            return c

        lax.fori_loop(0, units, to_bf16, 0)

    def ffn_body(off, m):
        xk = xs_scr[pl.ds(off, m), :]
        g = _dot(xk, wg_ref[0])
        act = (g * _sigmoid(g)) * _dot(xk, wu_ref[0])
        acc_scr[pl.ds(off, m), :] += _dot(act.astype(BF16), wd_ref[0])

    for_blocks(ffn_body)

    @pl.when(f == pl.num_programs(2) - 1)
    def _scatter():
        lane = lax.broadcasted_iota(jnp.int32, (sub, LANES), 1)
        lane_f = lane.astype(F32)
        reps = width // LANES
        for tb in range(nsub):
            sl = slice(tb * sub, (tb + 1) * sub)
            pos_e = jnp.sum(jnp.where(lane == e, posc_scr[sl, :], 0.0), axis=1, keepdims=True)
            w_e = jnp.sum(jnp.where(lane == e, wt_ref[sl, :], 0.0), axis=1, keepdims=True)
            posb_scr[...] = jnp.broadcast_to(pos_e, (sub, LANES)) - lane_f
            wb_scr[...] = jnp.broadcast_to(w_e, (sub, LANES))
            start, n_win = windows(tb)

            def scatter_window(k, c, start=start, sl=sl):
                off = pl.multiple_of(start + k * win, small)
                y = acc_scr[pl.ds(off, win), :].astype(BF16)
                rel = posb_scr[...] - off.astype(F32)
                onehot = jnp.concatenate(
                    [jnp.where(rel == float(g * LANES), 1.0, 0.0) for g in range(win // LANES)], axis=1).astype(BF16)
                o_ref[sl, :] += jnp.concatenate([wb_scr[...]] * reps, axis=1) * _dot(onehot, y)
                return c

            lax.fori_loop(0, n_win, scatter_window, 0)

        @pl.when(e == N_EXPERTS - 1)
        def _residual():
            o_ref[...] = x2_ref[...] + mod_ref[0, 5:6, :] * o_ref[...]


def _moe(h2, wts, mask, x2, mod, counts, cum_counts, wg, wu, wd, seq):
    t, d = h2.shape
    fdim = wg.shape[2]
    cols = MOE_COLS if fdim % MOE_COLS == 0 else fdim
    n_f = fdim // cols
    rows = min(MOE_ROWS, seq)
    once = pl.Buffered(1)
    tile_spec = lambda width: pl.BlockSpec((rows, width), lambda i, e, f, n, c: (i, 0), pipeline_mode=once)
    grid_spec = pltpu.PrefetchScalarGridSpec(
        num_scalar_prefetch=2,
        grid=(t // rows, N_EXPERTS, n_f),
        in_specs=[tile_spec(d), tile_spec(LANES), tile_spec(LANES), tile_spec(d),
                  pl.BlockSpec((1, 6, d), lambda i, e, f, n, c: (i * rows // seq, 0, 0)),
                  pl.BlockSpec((1, d, cols), lambda i, e, f, n, c: (e, 0, f)),
                  pl.BlockSpec((1, d, cols), lambda i, e, f, n, c: (e, 0, f)),
                  pl.BlockSpec((1, cols, d), lambda i, e, f, n, c: (e, f, 0))],
        out_specs=pl.BlockSpec((rows, d), lambda i, e, f, n, c: (i, 0), pipeline_mode=once),
        scratch_shapes=[pltpu.VMEM((rows, LANES), F32),
                        pltpu.VMEM((N_EXPERTS, rows), F32),
                        pltpu.VMEM((rows, d), BF16),
                        pltpu.VMEM((rows + MOE_WINDOW_ROWS, d), F32),
                        pltpu.VMEM((MOE_SCATTER_ROWS, LANES), F32),
                        pltpu.VMEM((MOE_SCATTER_ROWS, LANES), F32)],
    )
    return pl.pallas_call(
        _moe_kernel,
        grid_spec=grid_spec,
        out_shape=jax.ShapeDtypeStruct((t, d), F32),
        compiler_params=_params("arbitrary", "arbitrary", "arbitrary"),
        name="moe_experts",
    )(counts, cum_counts, h2, wts, mask, x2, mod, wg, wu, wd)


def _block_diag(w):
    h, i, j = w.shape
    eye = jnp.eye(h, dtype=w.dtype)
    return jnp.einsum('hij,hk->hikj', w, eye).reshape(h * i, h * j)


def _rope_tables(seq):
    half = GROUP_DIM // 2
    inv = ROPE_THETA ** (-jnp.arange(half, dtype=F32) / half)
    ang = jnp.arange(seq).astype(F32)[:, None] * inv[None, :]
    cos, sin, zero = jnp.cos(ang), jnp.sin(ang), jnp.zeros_like(ang)
    reps = LANES // GROUP_DIM
    return (jnp.tile(jnp.concatenate([cos, cos], axis=1), (1, reps)),
            jnp.tile(jnp.concatenate([-sin, zero], axis=1), (1, reps)),
            jnp.tile(jnp.concatenate([zero, sin], axis=1), (1, reps)))


def _row(v):
    return v.reshape(1, -1).astype(F32)


def kernel(x, c, e_ada_w, e_ada_b, e_norm_mix, e_norm_ffn, e_w_in, e_conv_a_w, e_conv_a_b,
           e_ln_a_g, e_ln_a_b, e_conv_b_w, e_conv_b_b, e_lru_wa, e_lru_ba, e_lru_wx, e_lru_bx,
           e_lru_lambda, e_w_out, e_ffn_wg, e_ffn_wu, e_ffn_wd,
           o_ada_w, o_ada_b, o_norm_mix, o_norm_ffn, o_w_in, o_conv_c_w, o_q_norm, o_k_norm,
           o_w_out, o_router_w, o_router_b, o_moe_wg, o_moe_wu, o_moe_wd):
    b, s, d = x.shape
    assert s % MOBA_BLOCK == 0 and d % LANES == 0
    c_pad = jnp.zeros((SUBLANES, d), F32).at[:b].set(c.astype(F32))
    seg = jnp.kron(jnp.eye(N_GROUPS, dtype=F32), jnp.ones((GROUP_DIM, GROUP_DIM), F32)).astype(BF16)
    cos, sin_a, sin_b = _rope_tables(s)

    def modulation(w, bias):
        return _modulation(c_pad, w, _row(bias))[:b].reshape(b, 6, d)

    x = x.astype(F32)
    for layer in range(DEPTH):
        j = layer // 2
        if layer % 2 == 0:
            mod = modulation(e_ada_w[j], e_ada_b[j])
            x = _even_mixer(
                x, mod, _row(e_norm_mix[j]), e_w_in[j].astype(BF16),
                e_conv_a_w[j].astype(F32), _row(e_conv_a_b[j]), _row(e_ln_a_g[j]), _row(e_ln_a_b[j]),
                e_conv_b_w[j].astype(F32), _row(e_conv_b_b[j]),
                _block_diag(e_lru_wa[j]).astype(BF16), _row(e_lru_ba[j]),
                _block_diag(e_lru_wx[j]).astype(BF16), _row(e_lru_bx[j]),
                _row(e_lru_lambda[j]), e_w_out[j].astype(BF16))
            x = _dense_ffn(x, mod, _row(e_norm_ffn[j]), e_ffn_wg[j].astype(BF16),
                           e_ffn_wu[j].astype(BF16), e_ffn_wd[j].astype(BF16))
        else:
            mod = modulation(o_ada_w[j], o_ada_b[j])
            yc, q, k, v, kmean = _odd_mixer(
                x, mod, _row(o_norm_mix[j]), o_w_in[j].astype(BF16), o_conv_c_w[j].astype(F32),
                _row(jnp.tile(o_q_norm[j], N_GROUPS)), _row(jnp.tile(o_k_norm[j], N_GROUPS)),
                seg, cos, sin_a, sin_b)
            yd, (moe_wg, moe_wu, moe_wd) = _moba(
                q, k, v, kmean, [o_moe_wg[j].astype(F32), o_moe_wu[j].astype(F32), o_moe_wd[j].astype(F32)])
            rw = jnp.zeros((d, LANES), F32).at[:, :N_EXPERTS].set(o_router_w[j].astype(F32))
            rw_hi = rw.astype(BF16)
            rw_lo = (rw - rw_hi.astype(F32)).astype(BF16)
            rb = jnp.zeros((1, LANES), F32).at[0, :N_EXPERTS].set(o_router_b[j].astype(F32))
            x2, h2, wts, mask, cnt = _odd_out(yc, yd, x, mod, o_w_out[j].astype(BF16),
                                              _row(o_norm_ffn[j]), rw_hi, rw_lo, rb)
            rows = min(MOE_ROWS, s)
            assert MIX_ROWS == MOE_SCATTER_ROWS or s < MIX_ROWS
            sub_cnt = cnt.reshape(b * s // rows, -1, LANES)[:, :, :N_EXPERTS].astype(jnp.int32)
            cum = jnp.cumsum(sub_cnt, axis=1)
            cum = jnp.concatenate([jnp.zeros_like(cum[:, :1]), cum], axis=1)
            counts = cum[:, -1, :].reshape(-1)
            cum_counts = jnp.transpose(cum, (0, 2, 1)).reshape(-1)
            out = _moe(h2.reshape(b * s, d), wts.reshape(b * s, LANES), mask.reshape(b * s, LANES),
                       x2.reshape(b * s, d), mod, counts, cum_counts,
                       moe_wg, moe_wu, moe_wd, s)
            x = out.reshape(b, s, d)
    return x
```

```python
import functools

import jax
import jax.numpy as jnp
from jax import lax
from jax.experimental import pallas as pl
from jax.experimental.pallas import tpu as pltpu

F32 = jnp.float32
BF16 = jnp.bfloat16

N_GROUPS = 8
GROUP_DIM = 64
MIX_W = N_GROUPS * GROUP_DIM
CONF_WIDTH = 31
LRU_CONV_WIDTH = 4
LRU_C = 8.0
SHORT_CONV_WIDTH = 3
MOBA_BLOCK = 256
MOBA_TOPK = 3
ROPE_THETA = 10000.0
N_EXPERTS = 8
EPS = 1e-6
DEPTH = 2

LANES = 128
SUBLANES = 8
VMEM_LIMIT_BYTES = 62 * 1024 * 1024

MIX_ROWS = 512
FFN_ROWS = 512
FFN_COLS = 2816
MOE_ROWS = 2048
MOE_COLS = 1792
MOE_CHUNK = 256
MOE_BIG_ROWS = 512
MOE_MED_ROWS = 256
MOE_SMALL_ROWS = 128
MOE_SCATTER_ROWS = 512
MOE_WINDOW_ROWS = 256
MOBA_KEY_GROUP = 4
CONV_A_HALO = 32
CONV_S_HALO = 8


def _dot(a, b):
    return jnp.dot(a, b, preferred_element_type=F32)


def _dot_nt(a, b):
    return lax.dot_general(a, b, (((1,), (1,)), ((), ())), preferred_element_type=F32)


def _sigmoid(x):
    return 1.0 / (1.0 + jnp.exp(-x))


def _rms_mod(x, gain, scale, shift):
    ms = jnp.mean(x * x, axis=-1, keepdims=True)
    return (x * lax.rsqrt(ms + EPS)) * gain * (1.0 + scale) + shift


def _params(*sem):
    return pltpu.CompilerParams(dimension_semantics=sem, vmem_limit_bytes=VMEM_LIMIT_BYTES)


def _full(shape):
    n = len(shape)
    return pl.BlockSpec(shape, lambda *_: (0,) * n)


def _mod_kernel(c_ref, w_ref, b_ref, o_ref):
    c = c_ref[...]
    c_act = c * _sigmoid(c)
    o_ref[...] = _dot(c_act.astype(BF16), w_ref[...].astype(BF16)) + b_ref[...]


def _modulation(c_pad, w, b):
    d, n = w.shape
    tn = n // 4
    return pl.pallas_call(
        _mod_kernel,
        grid=(4,),
        in_specs=[_full(c_pad.shape),
                  pl.BlockSpec((d, tn), lambda j: (0, j)),
                  pl.BlockSpec((1, tn), lambda j: (0, j))],
        out_specs=pl.BlockSpec((c_pad.shape[0], tn), lambda j: (0, j)),
        out_shape=jax.ShapeDtypeStruct((c_pad.shape[0], n), F32),
        compiler_params=_params("arbitrary"),
        name="adaln_mod",
    )(c_pad, w, b)


def _causal_conv(buf, halo, rows, w_ref, width):
    total = halo + rows
    full = buf[0:total, :]
    acc = None
    for phase in range(SUBLANES):
        taps = [j for j in range(width) if (halo - (width - 1) + j) % SUBLANES == phase]
        if not taps:
            continue
        shifted = full if phase == 0 else pltpu.roll(full, total - phase, 0)
        for j in taps:
            start = halo - (width - 1) + j - phase
            term = w_ref[j:j + 1, :] * shifted[start:start + rows, :]
            acc = term if acc is None else acc + term
    return acc


def _even_mixer_kernel(x_ref, mod_ref, gain_ref, win_ref, caw_ref, cab_ref, lng_ref, lnb_ref,
                       cbw_ref, cbb_ref, wa_ref, ba_ref, wx_ref, bx_ref, lam_ref, wout_ref,
                       o_ref, abuf, bbuf, hcar):
    rows = x_ref.shape[1]

    @pl.when(pl.program_id(1) == 0)
    def _():
        abuf[0:CONV_A_HALO, :] = jnp.zeros((CONV_A_HALO, MIX_W), F32)
        bbuf[0:CONV_S_HALO, :] = jnp.zeros((CONV_S_HALO, MIX_W), F32)
        hcar[...] = jnp.zeros(hcar.shape, F32)

    x = x_ref[0]
    h = _rms_mod(x, gain_ref[...], mod_ref[0, 1:2, :], mod_ref[0, 0:1, :])
    u = _dot(h.astype(BF16), win_ref[...])
    a_val = u[:, 0 * MIX_W:1 * MIX_W]
    a_gate = u[:, 1 * MIX_W:2 * MIX_W]
    b_x = u[:, 2 * MIX_W:3 * MIX_W]
    b_gate = u[:, 3 * MIX_W:4 * MIX_W]

    abuf[CONV_A_HALO:CONV_A_HALO + rows, :] = a_val * _sigmoid(a_gate)
    ya = _causal_conv(abuf, CONV_A_HALO, rows, caw_ref, CONF_WIDTH) + cab_ref[...]
    abuf[0:CONV_A_HALO, :] = abuf[rows:rows + CONV_A_HALO, :]
    mu = jnp.mean(ya, axis=-1, keepdims=True)
    dev = ya - mu
    var = jnp.mean(dev * dev, axis=-1, keepdims=True)
    ya = dev * lax.rsqrt(var + EPS) * lng_ref[...] + lnb_ref[...]
    ya = ya * _sigmoid(ya)

    bbuf[CONV_S_HALO:CONV_S_HALO + rows, :] = b_x
    xb = _causal_conv(bbuf, CONV_S_HALO, rows, cbw_ref, LRU_CONV_WIDTH) + cbb_ref[...]
    bbuf[0:CONV_S_HALO, :] = bbuf[rows:rows + CONV_S_HALO, :]
    xb16 = xb.astype(BF16)
    r_gate = _sigmoid(_dot(xb16, wa_ref[...]) + ba_ref[...])
    i_gate = _sigmoid(_dot(xb16, wx_ref[...]) + bx_ref[...])
    lam = lam_ref[...]
    log_sig = -(jnp.maximum(-lam, 0.0) + jnp.log1p(jnp.exp(-jnp.abs(lam))))
    log_a = LRU_C * r_gate * log_sig
    a = jnp.exp(log_a)
    mult = jnp.sqrt(-jnp.tanh(log_a) * (1.0 + a * a))
    bterm = mult * (i_gate * xb)

    row = lax.broadcasted_iota(jnp.int32, (rows, MIX_W), 0)
    d = 1
    while d < rows:
        keep = row >= d
        a_prev = jnp.where(keep, pltpu.roll(a, d, 0), 1.0)
        b_prev = jnp.where(keep, pltpu.roll(bterm, d, 0), 0.0)
        bterm = bterm + a * b_prev
        a = a * a_prev
        d *= 2
    hseq = bterm + a * hcar[0:1, :]
    hcar[...] = jnp.broadcast_to(hseq[rows - 1:rows, :], hcar.shape)

    gelu = 0.5 * b_gate * (1.0 + jnp.tanh(0.7978845608028654 * (b_gate + 0.044715 * (b_gate * b_gate * b_gate))))
    yb = hseq * gelu

    mix = _dot(ya.astype(BF16), wout_ref[0:MIX_W, :]) + _dot(yb.astype(BF16), wout_ref[MIX_W:2 * MIX_W, :])
    o_ref[0] = x + mod_ref[0, 2:3, :] * mix


def _even_mixer(x, mod, gain, w_in, caw, cab, lng, lnb, cbw, cbb, wa, ba, wx, bx, lam, w_out):
    b, s, d = x.shape
    rows = min(MIX_ROWS, s)
    row_spec = pl.BlockSpec((1, rows, d), lambda i, j: (i, j, 0))
    small = [gain, w_in, caw, cab, lng, lnb, cbw, cbb, wa, ba, wx, bx, lam, w_out]
    return pl.pallas_call(
        _even_mixer_kernel,
        grid=(b, s // rows),
        in_specs=[row_spec, pl.BlockSpec((1, 6, d), lambda i, j: (i, 0, 0))] + [_full(t.shape) for t in small],
        out_specs=row_spec,
        out_shape=jax.ShapeDtypeStruct(x.shape, F32),
        scratch_shapes=[pltpu.VMEM((CONV_A_HALO + rows, MIX_W), F32),
                        pltpu.VMEM((CONV_S_HALO + rows, MIX_W), F32),
                        pltpu.VMEM((SUBLANES, MIX_W), F32)],
        compiler_params=_params("arbitrary", "arbitrary"),
        name="even_mixer",
    )(x, mod, *small)


def _ffn_kernel(x_ref, mod_ref, gain_ref, wg_ref, wu_ref, wd_ref, o_ref, h_scr):
    @pl.when(pl.program_id(2) == 0)
    def _():
        x = x_ref[0]
        h_scr[...] = _rms_mod(x, gain_ref[...], mod_ref[0, 4:5, :], mod_ref[0, 3:4, :]).astype(BF16)
        o_ref[0] = x

    h = h_scr[...]
    g = _dot(h, wg_ref[...])
    act = (g * _sigmoid(g)) * _dot(h, wu_ref[...])
    o_ref[0] += mod_ref[0, 5:6, :] * _dot(act.astype(BF16), wd_ref[...])


def _dense_ffn(x, mod, gain, wg, wu, wd):
    b, s, d = x.shape
    f = wg.shape[1]
    rows = min(FFN_ROWS, s)
    cols = FFN_COLS if f % FFN_COLS == 0 else f
    row_spec = pl.BlockSpec((1, rows, d), lambda i, j, k: (i, j, 0))
    return pl.pallas_call(
        _ffn_kernel,
        grid=(b, s // rows, f // cols),
        in_specs=[row_spec,
                  pl.BlockSpec((1, 6, d), lambda i, j, k: (i, 0, 0)),
                  _full(gain.shape),
                  pl.BlockSpec((d, cols), lambda i, j, k: (0, k)),
                  pl.BlockSpec((d, cols), lambda i, j, k: (0, k)),
                  pl.BlockSpec((cols, d), lambda i, j, k: (k, 0))],
        out_specs=row_spec,
        out_shape=jax.ShapeDtypeStruct(x.shape, F32),
        scratch_shapes=[pltpu.VMEM((rows, d), BF16)],
        compiler_params=_params("arbitrary", "arbitrary", "arbitrary"),
        name="dense_ffn",
    )(x, mod, gain, wg, wu, wd)


def _odd_mixer_kernel(x_ref, mod_ref, gain_ref, win_ref, ccw_ref, qn_ref, kn_ref, seg_ref,
                      cos_ref, sa_ref, sb_ref, yc_ref, q_ref, k_ref, v_ref, km_ref, cbuf):
    rows = x_ref.shape[1]

    @pl.when(pl.program_id(1) == 0)
    def _():
        cbuf[0:CONV_S_HALO, :] = jnp.zeros((CONV_S_HALO, MIX_W), F32)

    x = x_ref[0]
    h = _rms_mod(x, gain_ref[...], mod_ref[0, 1:2, :], mod_ref[0, 0:1, :])
    u = _dot(h.astype(BF16), win_ref[...])
    c_h = u[:, 0 * MIX_W:1 * MIX_W]
    c_b = u[:, 1 * MIX_W:2 * MIX_W]
    c_c = u[:, 2 * MIX_W:3 * MIX_W]
    q = u[:, 3 * MIX_W:4 * MIX_W]
    k = u[:, 4 * MIX_W:5 * MIX_W]
    v = u[:, 5 * MIX_W:6 * MIX_W]

    cbuf[CONV_S_HALO:CONV_S_HALO + rows, :] = c_c * c_h
    conv = _causal_conv(cbuf, CONV_S_HALO, rows, ccw_ref, SHORT_CONV_WIDTH)
    cbuf[0:CONV_S_HALO, :] = cbuf[rows:rows + CONV_S_HALO, :]
    yc_ref[0] = (c_b * conv).astype(yc_ref.dtype)

    seg = seg_ref[...]
    cos = cos_ref[...]
    sin_a = sa_ref[...]
    sin_b = sb_ref[...]

    def head_norm_rope(t, gain):
        t2 = t * t
        hi = t2.astype(BF16)
        lo = (t2 - hi.astype(F32)).astype(BF16)
        ssq = _dot(hi, seg) + _dot(lo, seg)
        t = t * lax.rsqrt(ssq * (1.0 / GROUP_DIM) + EPS) * gain
        outs = []
        for g in range(MIX_W // LANES):
            tg = t[:, g * LANES:(g + 1) * LANES]
            outs.append(tg * cos + pltpu.roll(tg, LANES - GROUP_DIM // 2, 1) * sin_a
                        + pltpu.roll(tg, GROUP_DIM // 2, 1) * sin_b)
        return jnp.concatenate(outs, axis=1)

    qr = head_norm_rope(q, qn_ref[...])
    kr = head_norm_rope(k, kn_ref[...])
    q_ref[0] = (qr * (GROUP_DIM ** -0.5)).astype(q_ref.dtype)
    k_ref[0] = kr.astype(k_ref.dtype)
    v_ref[0] = v.astype(v_ref.dtype)
    for i in range(rows // MOBA_BLOCK):
        km_ref[0, i] = jnp.mean(kr[i * MOBA_BLOCK:(i + 1) * MOBA_BLOCK, :], axis=0, keepdims=True)


def _odd_mixer(x, mod, gain, w_in, ccw, qn, kn, seg, cos, sin_a, sin_b):
    b, s, d = x.shape
    rows = min(MIX_ROWS, s)
    nb = s // MOBA_BLOCK
    row_spec = pl.BlockSpec((1, rows, d), lambda i, j: (i, j, 0))
    mix_spec = pl.BlockSpec((1, rows, MIX_W), lambda i, j: (i, j, 0))
    tab_spec = pl.BlockSpec((rows, LANES), lambda i, j: (j, 0))
    small = [gain, w_in, ccw, qn, kn, seg]
    mix_shape = jax.ShapeDtypeStruct((b, s, MIX_W), BF16)
    return pl.pallas_call(
        _odd_mixer_kernel,
        grid=(b, s // rows),
        in_specs=[row_spec, pl.BlockSpec((1, 6, d), lambda i, j: (i, 0, 0))]
                 + [_full(t.shape) for t in small] + [tab_spec, tab_spec, tab_spec],
        out_specs=[mix_spec, mix_spec, mix_spec, mix_spec,
                   pl.BlockSpec((1, rows // MOBA_BLOCK, 1, MIX_W), lambda i, j: (i, j, 0, 0))],
        out_shape=[mix_shape, mix_shape, mix_shape, mix_shape,
                   jax.ShapeDtypeStruct((b, nb, 1, MIX_W), F32)],
        scratch_shapes=[pltpu.VMEM((CONV_S_HALO + rows, MIX_W), F32)],
        compiler_params=_params("arbitrary", "arbitrary"),
        name="odd_mixer",
    )(x, mod, *small, cos, sin_a, sin_b)


def _top_blocks(gate, own):
    nblk = gate.shape[0]
    blk = lax.broadcasted_iota(jnp.int32, gate.shape, 0)
    remaining = blk < own
    sel = jnp.zeros(gate.shape, F32)
    for _ in range(MOBA_TOPK):
        gm = jnp.where(remaining, gate, -jnp.inf)
        top = jnp.max(gm, axis=0, keepdims=True)
        cand = remaining & (gm == top)
        first = jnp.min(jnp.where(cand, blk, nblk), axis=0, keepdims=True)
        pick = cand & (blk == first)
        sel = jnp.where(pick, 1.0, sel)
        remaining = remaining & jnp.logical_not(pick)
    return sel


def _moba_kernel(q_ref, k_ref, v_ref, km_ref, *rest):
    n_cast = (len(rest) - 1) // 2
    o_ref = rest[n_cast]
    for src, dst in zip(rest[:n_cast], rest[n_cast + 1:]):
        dst[...] = src[...].astype(dst.dtype)
    group = MOBA_KEY_GROUP
    step = pl.program_id(2)
    nblk = km_ref.shape[1]
    blk_rows = MOBA_BLOCK
    q = q_ref[0]
    q_rows = q.shape[0]
    lane = lax.broadcasted_iota(jnp.int32, q.shape, 1)
    zero = jnp.zeros_like(q)
    kmean = km_ref[0, :, 0, :].astype(BF16)
    row = lax.broadcasted_iota(jnp.int32, (blk_rows, blk_rows), 0)
    col = lax.broadcasted_iota(jnp.int32, (blk_rows, blk_rows), 1)
    causal_bias = jnp.where(col <= row, 0.0, -jnp.inf)
    pad = jnp.zeros((LANES - nblk, q_rows), F32)
    q_pos = lax.broadcasted_iota(jnp.int32, (1, q_rows), 1)
    own = step * group + lax.shift_right_logical(q_pos, blk_rows.bit_length() - 1)

    q_heads, bias_heads = [], []
    for hh in range(LANES // GROUP_DIM):
        in_head = (lane >= hh * GROUP_DIM) & (lane < (hh + 1) * GROUP_DIM)
        qh = jnp.where(in_head, q, zero)
        sel_t = _top_blocks(_dot_nt(kmean, qh), own)
        sel = jnp.transpose(jnp.concatenate([sel_t, pad], axis=0))
        q_heads.append(qh)
        bias_heads.append(jnp.where(sel > 0.0, 0.0, -jnp.inf))

    for g in range(nblk // group):
        @pl.when(step == g)
        def _(g=g):
            outs = []
            for qh, bias in zip(q_heads, bias_heads):
                blocks = []
                for r in range(group):
                    nk = g * group + r + 1
                    q_sl = slice(r * blk_rows, (r + 1) * blk_rows)
                    s = _dot_nt(qh[q_sl, :], k_ref[0, 0:nk * blk_rows, :])
                    pieces = [s[:, jb * blk_rows:(jb + 1) * blk_rows] + bias[q_sl, jb:jb + 1] for jb in range(nk - 1)]
                    pieces.append(s[:, (nk - 1) * blk_rows:] + causal_bias)
                    s = jnp.concatenate(pieces, axis=1)
                    m = jnp.max(s, axis=1, keepdims=True)
                    p = jnp.exp(s - m)
                    l = jnp.sum(p, axis=1, keepdims=True)
                    blocks.append(_dot(p.astype(BF16), v_ref[0, 0:nk * blk_rows, :]) / l)
                outs.append(jnp.concatenate(blocks, axis=0))
            o_ref[0] = jnp.where(lane < GROUP_DIM, outs[0], outs[1]).astype(o_ref.dtype)


def _moba(q, k, v, kmean, cast_weights):
    b, s, _ = q.shape
    assert s % (MOBA_KEY_GROUP * MOBA_BLOCK) == 0
    nb = s // (MOBA_KEY_GROUP * MOBA_BLOCK)
    pairs = MIX_W // LANES
    steps = b * pairs * nb
    q_spec = pl.BlockSpec((1, MOBA_KEY_GROUP * MOBA_BLOCK, LANES), lambda i, p, j: (i, j, p))
    kv_spec = pl.BlockSpec((1, s, LANES), lambda i, p, j: (i, 0, p))
    bf16_rows = 2 * SUBLANES
    sliced = [w.reshape(steps, -1, w.shape[-1]) for w in cast_weights]
    assert all(w.shape[1] % bf16_rows == 0 and w.shape[2] % LANES == 0 for w in sliced)
    w_specs = [pl.BlockSpec((1,) + w.shape[1:], lambda i, p, j: ((i * pairs + p) * nb + j, 0, 0)) for w in sliced]
    outs = pl.pallas_call(
        _moba_kernel,
        grid=(b, pairs, nb),
        in_specs=[q_spec, kv_spec, kv_spec,
                  pl.BlockSpec((1, s // MOBA_BLOCK, 1, LANES), lambda i, p, j: (i, 0, 0, p))] + w_specs,
        out_specs=[q_spec] + w_specs,
        out_shape=[jax.ShapeDtypeStruct(q.shape, BF16)] + [jax.ShapeDtypeStruct(w.shape, BF16) for w in sliced],
        compiler_params=_params("arbitrary", "arbitrary", "arbitrary"),
        name="moba_attention",
    )(q, k, v, kmean, *sliced)
    return outs[0], [o.reshape(w.shape) for o, w in zip(outs[1:], cast_weights)]


def _odd_out_kernel(yc_ref, yd_ref, x_ref, mod_ref, wout_ref, gain_ref, rwh_ref, rwl_ref, rb_ref,
                    x2_ref, h2_ref, wt_ref, mk_ref, cnt_ref):
    x = x_ref[0]
    mix = _dot(yc_ref[0], wout_ref[0:MIX_W, :]) + _dot(yd_ref[0], wout_ref[MIX_W:2 * MIX_W, :])
    x2 = x + mod_ref[0, 2:3, :] * mix
    x2_ref[0] = x2
    h = _rms_mod(x2, gain_ref[...], mod_ref[0, 4:5, :], mod_ref[0, 3:4, :])
    h_hi = h.astype(BF16)
    h2_ref[0] = h_hi.astype(h2_ref.dtype)

    h_lo = (h - h_hi.astype(F32)).astype(BF16)
    rwh = rwh_ref[...]
    logits = _dot(h_hi, rwh) + _dot(h_lo, rwh) + _dot(h_hi, rwl_ref[...]) + rb_ref[...]
    lane = lax.broadcasted_iota(jnp.int32, logits.shape, 1)
    valid = lane < N_EXPERTS
    lg = jnp.where(valid, logits, -jnp.inf)
    top1 = jnp.max(lg, axis=1, keepdims=True)
    idx1 = jnp.min(jnp.where(lg == top1, lane, LANES), axis=1, keepdims=True)
    pick1 = lane == idx1
    lg2 = jnp.where(pick1, -jnp.inf, lg)
    top2 = jnp.max(lg2, axis=1, keepdims=True)
    idx2 = jnp.min(jnp.where((lg2 == top2) & valid & jnp.logical_not(pick1), lane, LANES), axis=1, keepdims=True)
    pick2 = lane == idx2
    e2 = jnp.exp(top2 - top1)
    den = 1.0 + e2
    wt_ref[0] = jnp.where(pick1, 1.0 / den, 0.0) + jnp.where(pick2, e2 / den, 0.0)
    mask = jnp.where(pick1 | pick2, 1.0, 0.0)
    mk_ref[0] = mask.astype(mk_ref.dtype)
    cnt_ref[0, 0] = jnp.sum(mask, axis=0, keepdims=True)


def _odd_out(yc, yd, x, mod, w_out, gain, rw_hi, rw_lo, rb):
    b, s, d = x.shape
    rows = min(MIX_ROWS, s)
    row_spec = pl.BlockSpec((1, rows, d), lambda i, j: (i, j, 0))
    mix_spec = pl.BlockSpec((1, rows, MIX_W), lambda i, j: (i, j, 0))
    lane_spec = pl.BlockSpec((1, rows, LANES), lambda i, j: (i, j, 0))
    small = [w_out, gain, rw_hi, rw_lo, rb]
    return pl.pallas_call(
        _odd_out_kernel,
        grid=(b, s // rows),
        in_specs=[mix_spec, mix_spec, row_spec, pl.BlockSpec((1, 6, d), lambda i, j: (i, 0, 0))]
                 + [_full(t.shape) for t in small],
        out_specs=[row_spec, row_spec, lane_spec, lane_spec,
                   pl.BlockSpec((1, 1, 1, LANES), lambda i, j: (i, j, 0, 0))],
        out_shape=[jax.ShapeDtypeStruct(x.shape, F32),
                   jax.ShapeDtypeStruct(x.shape, BF16),
                   jax.ShapeDtypeStruct((b, s, LANES), F32),
                   jax.ShapeDtypeStruct((b, s, LANES), BF16),
                   jax.ShapeDtypeStruct((b, s // rows, 1, LANES), F32)],
        compiler_params=_params("arbitrary", "arbitrary"),
        name="odd_out_router",
    )(yc, yd, x, mod, *small)


def _moe_kernel(cnt_ref, cume_ref, h2_ref, wt_ref, mk_ref, x2_ref, mod_ref, wg_ref, wu_ref, wd_ref, o_ref,
                posc_scr, posr_scr, xs_scr, acc_scr, posb_scr, wb_scr):
    tile = pl.program_id(0)
    e = pl.program_id(1)
    f = pl.program_id(2)
    rows = h2_ref.shape[0]
    ch = MOE_CHUNK
    big, med, small = MOE_BIG_ROWS, MOE_MED_ROWS, MOE_SMALL_ROWS
    count = cnt_ref[tile * N_EXPERTS + e]
    units = lax.shift_right_logical(count + (small - 1), small.bit_length() - 1)
    n_big = lax.shift_right_logical(units, 2)
    n_med = lax.shift_right_logical(units & 3, 1)
    n_small = units & 1
    med_base = n_big * big
    small_base = med_base + n_med * med

    def for_blocks(body):
        lax.fori_loop(0, n_big, lambda kk, c: body(pl.multiple_of(kk * big, big), big) or c, 0)
        lax.fori_loop(0, n_med, lambda kk, c: body(pl.multiple_of(med_base, med), med) or c, 0)
        lax.fori_loop(0, n_small, lambda kk, c: body(pl.multiple_of(small_base, small), small) or c, 0)

    @pl.when((e == 0) & (f == 0))
    def _positions():
        o_ref[...] = jnp.zeros(o_ref.shape, F32)
        r_i = lax.broadcasted_iota(jnp.int32, (ch, ch), 0)
        c_i = lax.broadcasted_iota(jnp.int32, (ch, ch), 1)
        lower = jnp.where(c_i < r_i, 1.0, 0.0).astype(BF16)
        upper = jnp.where(r_i < c_i, 1.0, 0.0).astype(BF16)
        eye = jnp.where(lax.broadcasted_iota(jnp.int32, (N_EXPERTS, LANES), 0)
                        == lax.broadcasted_iota(jnp.int32, (N_EXPERTS, LANES), 1), 1.0, 0.0).astype(BF16)
        carry_c = jnp.zeros((1, LANES), F32)
        carry_r = jnp.zeros((N_EXPERTS, 1), F32)
        for blk in range(rows // ch):
            sl = slice(blk * ch, (blk + 1) * ch)
            mb = mk_ref[sl, :]
            mbf = mb.astype(F32)
            posc_scr[sl, :] = jnp.where(mbf > 0.0, _dot(lower, mb) + carry_c, -1.0)
            carry_c = carry_c + jnp.sum(mbf, axis=0, keepdims=True)
            mbt = _dot_nt(eye, mb)
            posr_scr[:, sl] = jnp.where(mbt > 0.0, _dot(mbt.astype(BF16), upper) + carry_r, -1.0)
            carry_r = carry_r + jnp.sum(mbt, axis=1, keepdims=True)

    sub = MOE_SCATTER_ROWS
    nsub = rows // sub
    win = MOE_WINDOW_ROWS
    cum_base = (tile * N_EXPERTS + e) * (nsub + 1)
    width = acc_scr.shape[1]

    def windows(tb):
        lo = cume_ref[cum_base + tb]
        hi = cume_ref[cum_base + tb + 1]
        start = lax.shift_left(lax.shift_right_logical(lo, small.bit_length() - 1), small.bit_length() - 1)
        n_win = lax.shift_right_logical(hi - start + (win - 1), win.bit_length() - 1)
        return start, jnp.where(hi > lo, n_win, 0)

    def unit_rows(u):
        return pl.ds(pl.multiple_of(u * small, small), small)

    @pl.when(f == 0)
    def _gather():
        def zero(u, c):
            acc_scr[unit_rows(u), :] = jnp.zeros((small, width), F32)
            return c

        lax.fori_loop(0, units + win // small, zero, 0)
        r_i = lax.broadcasted_iota(jnp.int32, (win, sub), 0).astype(F32)
        for tb in range(nsub):
            start, n_win = windows(tb)
            posr = posr_scr[pl.ds(e, 1), tb * sub:(tb + 1) * sub]
            h2_tb = h2_ref[tb * sub:(tb + 1) * sub, :]

            def gather_window(k, c, start=start, posr=posr, h2_tb=h2_tb):
                off = pl.multiple_of(start + k * win, small)
                onehot = jnp.where(posr - off.astype(F32) == r_i, 1.0, 0.0).astype(BF16)
                acc_scr[pl.ds(off, win), :] += _dot(onehot, h2_tb)
                return c

            lax.fori_loop(0, n_win, gather_window, 0)

        def to_bf16(u, c):
            xs_scr[unit_rows(u), :] = acc_scr[unit_rows(u), :].astype(xs_scr.dtype)
            acc_scr[unit_rows(u), :] = jnp.zeros((small, width), F32)
            return c

        lax.fori_loop(0, units, to_bf16, 0)

    def ffn_body(off, m):
        xk = xs_scr[pl.ds(off, m), :]
        g = _dot(xk, wg_ref[0])
        act = (g * _sigmoid(g)) * _dot(xk, wu_ref[0])
        acc_scr[pl.ds(off, m), :] += _dot(act.astype(BF16), wd_ref[0])

    for_blocks(ffn_body)

    @pl.when(f == pl.num_programs(2) - 1)
    def _scatter():
        lane = lax.broadcasted_iota(jnp.int32, (sub, LANES), 1)
        lane_f = lane.astype(F32)
        reps = width // LANES
        for tb in range(nsub):
            sl = slice(tb * sub, (tb + 1) * sub)
            pos_e = jnp.sum(jnp.where(lane == e, posc_scr[sl, :], 0.0), axis=1, keepdims=True)
            w_e = jnp.sum(jnp.where(lane == e, wt_ref[sl, :], 0.0), axis=1, keepdims=True)
            posb_scr[...] = jnp.broadcast_to(pos_e, (sub, LANES)) - lane_f
            wb_scr[...] = jnp.broadcast_to(w_e, (sub, LANES))
            start, n_win = windows(tb)

            def scatter_window(k, c, start=start, sl=sl):
                off = pl.multiple_of(start + k * win, small)
                y = acc_scr[pl.ds(off, win), :].astype(BF16)
                rel = posb_scr[...] - off.astype(F32)
                onehot = jnp.concatenate(
                    [jnp.where(rel == float(g * LANES), 1.0, 0.0) for g in range(win // LANES)], axis=1).astype(BF16)
                o_ref[sl, :] += jnp.concatenate([wb_scr[...]] * reps, axis=1) * _dot(onehot, y)
                return c

            lax.fori_loop(0, n_win, scatter_window, 0)

        @pl.when(e == N_EXPERTS - 1)
        def _residual():
            o_ref[...] = x2_ref[...] + mod_ref[0, 5:6, :] * o_ref[...]


def _moe(h2, wts, mask, x2, mod, counts, cum_counts, wg, wu, wd, seq):
    t, d = h2.shape
    fdim = wg.shape[2]
    cols = MOE_COLS if fdim % MOE_COLS == 0 else fdim
    n_f = fdim // cols
    rows = min(MOE_ROWS, seq)
    once = pl.Buffered(1)
    tile_spec = lambda width: pl.BlockSpec((rows, width), lambda i, e, f, n, c: (i, 0), pipeline_mode=once)
    grid_spec = pltpu.PrefetchScalarGridSpec(
        num_scalar_prefetch=2,
        grid=(t // rows, N_EXPERTS, n_f),
        in_specs=[tile_spec(d), tile_spec(LANES), tile_spec(LANES), tile_spec(d),
                  pl.BlockSpec((1, 6, d), lambda i, e, f, n, c: (i * rows // seq, 0, 0)),
                  pl.BlockSpec((1, d, cols), lambda i, e, f, n, c: (e, 0, f)),
                  pl.BlockSpec((1, d, cols), lambda i, e, f, n, c: (e, 0, f)),
                  pl.BlockSpec((1, cols, d), lambda i, e, f, n, c: (e, f, 0))],
        out_specs=pl.BlockSpec((rows, d), lambda i, e, f, n, c: (i, 0), pipeline_mode=once),
        scratch_shapes=[pltpu.VMEM((rows, LANES), F32),
                        pltpu.VMEM((N_EXPERTS, rows), F32),
                        pltpu.VMEM((rows, d), BF16),
                        pltpu.VMEM((rows + MOE_WINDOW_ROWS, d), F32),
                        pltpu.VMEM((MOE_SCATTER_ROWS, LANES), F32),
                        pltpu.VMEM((MOE_SCATTER_ROWS, LANES), F32)],
    )
    return pl.pallas_call(
        _moe_kernel,
        grid_spec=grid_spec,
        out_shape=jax.ShapeDtypeStruct((t, d), F32),
        compiler_params=_params("arbitrary", "arbitrary", "arbitrary"),
        name="moe_experts",
    )(counts, cum_counts, h2, wts, mask, x2, mod, wg, wu, wd)


def _block_diag(w):
    h, i, j = w.shape
    eye = jnp.eye(h, dtype=w.dtype)
    return jnp.einsum('hij,hk->hikj', w, eye).reshape(h * i, h * j)


def _rope_tables(seq):
    half = GROUP_DIM // 2
    inv = ROPE_THETA ** (-jnp.arange(half, dtype=F32) / half)
    ang = jnp.arange(seq).astype(F32)[:, None] * inv[None, :]
    cos, sin, zero = jnp.cos(ang), jnp.sin(ang), jnp.zeros_like(ang)
    reps = LANES // GROUP_DIM
    return (jnp.tile(jnp.concatenate([cos, cos], axis=1), (1, reps)),
            jnp.tile(jnp.concatenate([-sin, zero], axis=1), (1, reps)),
            jnp.tile(jnp.concatenate([zero, sin], axis=1), (1, reps)))


def _row(v):
    return v.reshape(1, -1).astype(F32)


def kernel(x, c, e_ada_w, e_ada_b, e_norm_mix, e_norm_ffn, e_w_in, e_conv_a_w, e_conv_a_b,
           e_ln_a_g, e_ln_a_b, e_conv_b_w, e_conv_b_b, e_lru_wa, e_lru_ba, e_lru_wx, e_lru_bx,
           e_lru_lambda, e_w_out, e_ffn_wg, e_ffn_wu, e_ffn_wd,
           o_ada_w, o_ada_b, o_norm_mix, o_norm_ffn, o_w_in, o_conv_c_w, o_q_norm, o_k_norm,
           o_w_out, o_router_w, o_router_b, o_moe_wg, o_moe_wu, o_moe_wd):
    b, s, d = x.shape
    assert s % MOBA_BLOCK == 0 and d % LANES == 0
    c_pad = jnp.zeros((SUBLANES, d), F32).at[:b].set(c.astype(F32))
    seg = jnp.kron(jnp.eye(N_GROUPS, dtype=F32), jnp.ones((GROUP_DIM, GROUP_DIM), F32)).astype(BF16)
    cos, sin_a, sin_b = _rope_tables(s)

    def modulation(w, bias):
        return _modulation(c_pad, w, _row(bias))[:b].reshape(b, 6, d)

    x = x.astype(F32)
    for layer in range(DEPTH):
        j = layer // 2
        if layer % 2 == 0:
            mod = modulation(e_ada_w[j], e_ada_b[j])
            x = _even_mixer(
                x, mod, _row(e_norm_mix[j]), e_w_in[j].astype(BF16),
                e_conv_a_w[j].astype(F32), _row(e_conv_a_b[j]), _row(e_ln_a_g[j]), _row(e_ln_a_b[j]),
                e_conv_b_w[j].astype(F32), _row(e_conv_b_b[j]),
                _block_diag(e_lru_wa[j]).astype(BF16), _row(e_lru_ba[j]),
                _block_diag(e_lru_wx[j]).astype(BF16), _row(e_lru_bx[j]),
                _row(e_lru_lambda[j]), e_w_out[j].astype(BF16))
            x = _dense_ffn(x, mod, _row(e_norm_ffn[j]), e_ffn_wg[j].astype(BF16),
                           e_ffn_wu[j].astype(BF16), e_ffn_wd[j].astype(BF16))
        else:
            mod = modulation(o_ada_w[j], o_ada_b[j])
            yc, q, k, v, kmean = _odd_mixer(
                x, mod, _row(o_norm_mix[j]), o_w_in[j].astype(BF16), o_conv_c_w[j].astype(F32),
                _row(jnp.tile(o_q_norm[j], N_GROUPS)), _row(jnp.tile(o_k_norm[j], N_GROUPS)),
                seg, cos, sin_a, sin_b)
            yd, (moe_wg, moe_wu, moe_wd) = _moba(
                q, k, v, kmean, [o_moe_wg[j].astype(F32), o_moe_wu[j].astype(F32), o_moe_wd[j].astype(F32)])
            rw = jnp.zeros((d, LANES), F32).at[:, :N_EXPERTS].set(o_router_w[j].astype(F32))
            rw_hi = rw.astype(BF16)
            rw_lo = (rw - rw_hi.astype(F32)).astype(BF16)
            rb = jnp.zeros((1, LANES), F32).at[0, :N_EXPERTS].set(o_router_b[j].astype(F32))
            x2, h2, wts, mask, cnt = _odd_out(yc, yd, x, mod, o_w_out[j].astype(BF16),
                                              _row(o_norm_ffn[j]), rw_hi, rw_lo, rb)
            rows = min(MOE_ROWS, s)
            assert MIX_ROWS == MOE_SCATTER_ROWS or s < MIX_ROWS
            sub_cnt = cnt.reshape(b * s // rows, -1, LANES)[:, :, :N_EXPERTS].astype(jnp.int32)
            cum = jnp.cumsum(sub_cnt, axis=1)
            cum = jnp.concatenate([jnp.zeros_like(cum[:, :1]), cum], axis=1)
            counts = cum[:, -1, :].reshape(-1)
            cum_counts = jnp.transpose(cum, (0, 2, 1)).reshape(-1)
            out = _moe(h2.reshape(b * s, d), wts.reshape(b * s, LANES), mask.reshape(b * s, LANES),
                       x2.reshape(b * s, d), mod, counts, cum_counts,
                       moe_wg, moe_wu, moe_wd, s)
            x = out.reshape(b, s, d)
    return x
```
